```python
import functools
import jax, jax.numpy as jnp
from jax import lax
import numpy as np

D_MODEL = 1024
BATCH = 8
SEQ = 2048
DEPTH = 2
DEC_BATCH = 32
DEC_SEQ = 1
PAST_LEN = 8192
PAGE_SIZE = 128

HEAD_DIM = 64
N_HEADS = D_MODEL // HEAD_DIM
H_A = N_HEADS // 4
H_B = N_HEADS // 4
H_C = N_HEADS - H_A - H_B
W_A = H_A * HEAD_DIM
W_B = H_B * HEAD_DIM
W_C = H_C * HEAD_DIM
D_MIX = W_A + W_B + W_C
D_IN = 3 * W_A + 2 * W_B + 3 * W_C + H_C
SPLIT_OFFSETS = (W_A, 2 * W_A, 3 * W_A, 3 * W_A + W_B, 3 * W_A + 2 * W_B,
                 3 * W_A + 2 * W_B + W_C, 3 * W_A + 2 * W_B + 2 * W_C, 3 * W_A + 2 * W_B + 3 * W_C)
CONV_W = 3
CHUNK = 128
Q_BLOCK = 128
N_GROUPS = 4
EXPERTS_PER_GROUP = 8
TOP_K = 2
D_EXPERT = D_MODEL // 2
D_PLE = 256
FORGET_BIAS_LO = 3.0
FORGET_BIAS_HI = 10.0
EPS = 1e-6

kernel_name = 'hymba_conv_gmlp_fox_hmoe_step'


def rmsnorm(x, g):
    xf = x.astype(jnp.float32)
    y = xf * lax.rsqrt(jnp.mean(xf * xf, axis=-1, keepdims=True) + EPS)
    return (y * g.astype(jnp.float32)).astype(x.dtype)


def layernorm(x, g, b):
    xf = x.astype(jnp.float32)
    mu = jnp.mean(xf, axis=-1, keepdims=True)
    var = jnp.mean(jnp.square(xf - mu), axis=-1, keepdims=True)
    y = (xf - mu) * lax.rsqrt(var + EPS)
    return (y * g.astype(jnp.float32) + b.astype(jnp.float32)).astype(x.dtype)


def short_conv_mixer(b_gate, c_gate, h, conv_w, buf):
    z = c_gate * h
    L = z.shape[1]
    zp = jnp.concatenate([buf.astype(z.dtype), z], axis=1)
    y = sum(conv_w[j] * zp[:, j:j + L] for j in range(CONV_W))
    return b_gate * y, zp[:, L:]


def chunk_gmlp_mixer(u, v, ln_g, ln_b, w_s, b_s):
    vn = layernorm(v, ln_g, ln_b)
    n, L, _ = vn.shape
    c = min(L, CHUNK)
    nc = L // c
    causal = jnp.tril(jnp.ones((CHUNK, CHUNK), dtype=bool))
    ws = jnp.where(causal, w_s, 0)[:, :c, :c]
    vh = vn.reshape(n, nc, c, H_B, HEAD_DIM)
    s = jnp.einsum('hts,ncshd->ncthd', ws.astype(vh.dtype), vh) + b_s[:, :c].T[:, :, None]
    return u * s.reshape(n, L, W_B), vn


def fox_prompt(q, k, v, logf):
    n, L, h, d = q.shape
    scale = d ** -0.5
    c = jnp.cumsum(logf, axis=1)
    nb = L // Q_BLOCK
    qb = q.reshape(n, nb, Q_BLOCK, h, d).transpose(1, 0, 2, 3, 4)
    cb = c.reshape(n, nb, Q_BLOCK, h).transpose(1, 0, 2, 3)
    starts = jnp.arange(nb, dtype=jnp.int32) * Q_BLOCK
    kpos = jnp.arange(L, dtype=jnp.int32)
    c_k = c.transpose(0, 2, 1)[:, :, None, :]

    def block(args):
        qi, ci, start = args
        s = jnp.einsum('nqhd,nkhd->nhqk', qi, k).astype(jnp.float32) * scale
        s = s + ci.transpose(0, 2, 1)[..., None] - c_k
        qpos = start + jnp.arange(Q_BLOCK, dtype=jnp.int32)
        s = jnp.where(kpos[None, :] <= qpos[:, None], s, -jnp.inf)
        p = jax.nn.softmax(s, axis=-1)
        return jnp.einsum('nhqk,nkhd->nqhd', p.astype(v.dtype), v)

    o = lax.map(block, (qb, cb, starts))
    return o.transpose(1, 0, 2, 3, 4).reshape(n, L, h * d)


def fox_sample(q, k, v, logf, k_past, v_past, logf_past):
    n, L, h, d = q.shape
    P = k_past.shape[1]
    scale = d ** -0.5
    lfp = logf_past.astype(jnp.float32)
    suffix = lax.cumsum(lfp, axis=1, reverse=True) - lfp
    cnT = jnp.cumsum(logf, axis=1).transpose(0, 2, 1)
    s_past = jnp.einsum('nqhd,nkhd->nhqk', q, k_past).astype(jnp.float32) * scale
    s_past = s_past + cnT[..., None] + suffix.transpose(0, 2, 1)[:, :, None, :]
    s_new = jnp.einsum('nqhd,nkhd->nhqk', q, k).astype(jnp.float32) * scale
    s_new = s_new + cnT[..., None] - cnT[:, :, None, :]
    causal = jnp.tril(jnp.ones((L, L), dtype=bool))
    s_new = jnp.where(causal, s_new, -jnp.inf)
    p = jax.nn.softmax(jnp.concatenate([s_past, s_new], axis=-1), axis=-1).astype(v.dtype)
    o = (jnp.einsum('nhqk,nkhd->nqhd', p[..., :P], v_past)
         + jnp.einsum('nhqk,nkhd->nqhd', p[..., P:], v))
    return o.reshape(n, L, h * d)


def hier_moe(h, w_grp, b_grp, w_rt, b_rt, w_gate, w_up, w_down):
    grp_prob = jax.nn.softmax((h @ w_grp).astype(jnp.float32) + b_grp.astype(jnp.float32), axis=-1)
    p_g, g_idx = lax.top_k(grp_prob, 1)
    exp_logits = jnp.einsum('nld,dge->nlge', h, w_rt).astype(jnp.float32) + b_rt.astype(jnp.float32)
    exp_logits = jnp.take_along_axis(exp_logits, g_idx[..., None], axis=2)[:, :, 0]
    p_e, e_idx = lax.top_k(jax.nn.softmax(exp_logits, axis=-1), TOP_K)
    w = p_g * p_e / jnp.sum(p_e, axis=-1, keepdims=True)
    fine = jnp.sum(jax.nn.one_hot(e_idx, EXPERTS_PER_GROUP, dtype=jnp.float32) * w[..., None], axis=-2)
    gates = (jax.nn.one_hot(g_idx[..., 0], N_GROUPS, dtype=jnp.float32)[..., None]
             * fine[:, :, None, :]).astype(h.dtype)
    y = jnp.zeros_like(h)
    for g in range(N_GROUPS):
        a = jnp.einsum('nld,edf->nlef', h, w_gate[g])
        b = jnp.einsum('nld,edf->nlef', h, w_up[g])
        act = jax.nn.silu(a) * b * gates[:, :, g, :, None]
        y = y + jnp.einsum('nlef,efd->nld', act, w_down[g])
    return y


def trunk_layer(x, p_i, lw, conv_buf, attend):
    n, L, _ = x.shape
    h = rmsnorm(x, lw['g_mix'])
    proj = h @ lw['w_in']
    b_a, c_a, h_a, u_b, v_b, q_c, k_c, v_c, f_c = jnp.split(proj, SPLIT_OFFSETS, axis=-1)
    y_a, new_buf = short_conv_mixer(b_a, c_a, h_a, lw['conv_w'], conv_buf)
    y_b, vn = chunk_gmlp_mixer(u_b, v_b, lw['ln_g'], lw['ln_b'], lw['w_s'], lw['b_s'])
    q = q_c.reshape(n, L, H_C, HEAD_DIM)
    k = k_c.reshape(n, L, H_C, HEAD_DIM)
    v = v_c.reshape(n, L, H_C, HEAD_DIM)
    logf = jax.nn.log_sigmoid(f_c.astype(jnp.float32) + lw['b_f'].astype(jnp.float32))
    y_c = attend(q, k, v, logf)
    g = lw['g_out']
    mixed = jnp.concatenate([rmsnorm(y_a, g[:W_A]), rmsnorm(y_b, g[W_A:W_A + W_B]),
                             rmsnorm(y_c, g[W_A + W_B:])], axis=-1)
    x = x + mixed @ lw['w_o']
    x = x + hier_moe(rmsnorm(x, lw['g_ffn']), lw['w_grp'], lw['b_grp'], lw['w_rt'], lw['b_rt'],
                     lw['w_gate'], lw['w_up'], lw['w_down'])
    gate = jax.nn.sigmoid(rmsnorm(x, lw['g_ple']) @ lw['w_ple_gate'])
    x = x + (p_i.astype(x.dtype) @ lw['w_ple']) * gate
    return x, k, v, logf, new_buf, vn


def setup_inputs(seed: int = 0) -> dict:
    key = jax.random.key(seed)
    keys = jax.random.split(key, 32)

    def nrm(i, shape, scale=1.0):
        return jax.random.normal(keys[i], shape, jnp.float32) * scale

    n_pages = PAST_LEN // PAGE_SIZE
    n_used = DEC_BATCH * n_pages
    n_pool = n_used + (n_used + 3) // 4
    page_table = jax.random.permutation(keys[0], n_pool)[:n_used].reshape(DEC_BATCH, n_pages).astype(jnp.int32)
    head_bias = jnp.linspace(FORGET_BIAS_LO, FORGET_BIAS_HI, H_C, dtype=jnp.float32)
    return {
        'x_prompt': nrm(1, (BATCH, SEQ, D_MODEL)),
        'x_sample': nrm(2, (DEC_BATCH, DEC_SEQ, D_MODEL)),
        'p_prompt': nrm(3, (DEPTH, BATCH, SEQ, D_PLE)),
        'p_sample': nrm(4, (DEPTH, DEC_BATCH, DEC_SEQ, D_PLE)),
        'cache_k': nrm(5, (DEPTH, n_pool, PAGE_SIZE, H_C, HEAD_DIM)),
        'cache_v': nrm(6, (DEPTH, n_pool, PAGE_SIZE, H_C, HEAD_DIM)),
        'cache_logf': jax.nn.log_sigmoid(head_bias + nrm(7, (DEPTH, n_pool, PAGE_SIZE, H_C), 0.5)),
        'state_conv': nrm(8, (DEPTH, DEC_BATCH, CONV_W - 1, W_A)),
        'page_table': page_table,
        'g_mix': 1.0 + nrm(9, (DEPTH, D_MODEL), 0.02),
        'w_in': nrm(10, (DEPTH, D_MODEL, D_IN), D_MODEL ** -0.5),
        'conv_w': nrm(11, (DEPTH, CONV_W, W_A), CONV_W ** -0.5),
        'ln_g': 1.0 + nrm(12, (DEPTH, W_B), 0.02),
        'ln_b': nrm(13, (DEPTH, W_B), 0.02),
        'w_s': nrm(14, (DEPTH, H_B, CHUNK, CHUNK), CHUNK ** -0.5),
        'b_s': 1.0 + nrm(15, (DEPTH, H_B, CHUNK), 0.02),
        'b_f': head_bias + nrm(16, (DEPTH, H_C), 0.1),
        'g_out': 1.0 + nrm(17, (DEPTH, D_MIX), 0.02),
        'w_o': nrm(18, (DEPTH, D_MIX, D_MODEL), D_MIX ** -0.5),
        'g_ffn': 1.0 + nrm(19, (DEPTH, D_MODEL), 0.02),
        'w_grp': nrm(20, (DEPTH, D_MODEL, N_GROUPS), D_MODEL ** -0.5),
        'b_grp': nrm(21, (DEPTH, N_GROUPS), 0.01),
        'w_rt': nrm(22, (DEPTH, D_MODEL, N_GROUPS, EXPERTS_PER_GROUP), D_MODEL ** -0.5),
        'b_rt': nrm(23, (DEPTH, N_GROUPS, EXPERTS_PER_GROUP), 0.01),
        'w_gate': nrm(24, (DEPTH, N_GROUPS, EXPERTS_PER_GROUP, D_MODEL, D_EXPERT), D_MODEL ** -0.5),
        'w_up': nrm(25, (DEPTH, N_GROUPS, EXPERTS_PER_GROUP, D_MODEL, D_EXPERT), D_MODEL ** -0.5),
        'w_down': nrm(26, (DEPTH, N_GROUPS, EXPERTS_PER_GROUP, D_EXPERT, D_MODEL), D_EXPERT ** -0.5),
        'g_ple': 1.0 + nrm(27, (DEPTH, D_MODEL), 0.02),
        'w_ple': nrm(28, (DEPTH, D_PLE, D_MODEL), D_PLE ** -0.5),
        'w_ple_gate': nrm(29, (DEPTH, D_MODEL, D_MODEL), D_MODEL ** -0.5),
        'g_final': 1.0 + nrm(30, (D_MODEL,), 0.02),
    }


def reference(x_prompt, x_sample, p_prompt, p_sample, cache_k, cache_v, cache_logf, state_conv,
              page_table, g_mix, w_in, conv_w, ln_g, ln_b, w_s, b_s, b_f, g_out, w_o, g_ffn,
              w_grp, b_grp, w_rt, b_rt, w_gate, w_up, w_down, g_ple, w_ple, w_ple_gate, g_final):
    n_dec = page_table.shape[0]
    xp, xs = x_prompt, x_sample
    kp_l, vp_l, lfp_l, cvp_l = [], [], [], []
    ks_l, vs_l, lfs_l, cvs_l, chv_l = [], [], [], [], []
    for i in range(DEPTH):
        lw = dict(g_mix=g_mix[i], w_in=w_in[i], conv_w=conv_w[i], ln_g=ln_g[i], ln_b=ln_b[i],
                  w_s=w_s[i], b_s=b_s[i], b_f=b_f[i], g_out=g_out[i], w_o=w_o[i], g_ffn=g_ffn[i],
                  w_grp=w_grp[i], b_grp=b_grp[i], w_rt=w_rt[i], b_rt=b_rt[i], w_gate=w_gate[i],
                  w_up=w_up[i], w_down=w_down[i], g_ple=g_ple[i], w_ple=w_ple[i],
                  w_ple_gate=w_ple_gate[i])
        buf0 = jnp.zeros((xp.shape[0], CONV_W - 1, W_A), xp.dtype)
        xp, k, v, lf, buf, _ = trunk_layer(xp, p_prompt[i], lw, buf0, fox_prompt)
        kp_l.append(k)
        vp_l.append(v)
        lfp_l.append(lf)
        cvp_l.append(buf)
        k_past = cache_k[i, page_table].reshape(n_dec, -1, H_C, HEAD_DIM)
        v_past = cache_v[i, page_table].reshape(n_dec, -1, H_C, HEAD_DIM)
        lf_past = cache_logf[i, page_table].reshape(n_dec, -1, H_C)
        attend = functools.partial(fox_sample, k_past=k_past, v_past=v_past, logf_past=lf_past)
        xs, k, v, lf, buf, vn = trunk_layer(xs, p_sample[i], lw, state_conv[i], attend)
        ks_l.append(k)
        vs_l.append(v)
        lfs_l.append(lf)
        cvs_l.append(buf)
        chv_l.append(vn)
    y_prompt = rmsnorm(xp, g_final)
    y_sample = rmsnorm(xs, g_final)
    return (y_prompt, y_sample, jnp.stack(kp_l), jnp.stack(vp_l), jnp.stack(lfp_l), jnp.stack(cvp_l),
            jnp.stack(ks_l), jnp.stack(vs_l), jnp.stack(lfs_l), jnp.stack(cvs_l), jnp.stack(chv_l))
```

```python
import functools

import jax
import jax.numpy as jnp
from jax import lax
from jax.experimental import pallas as pl
from jax.experimental.pallas import tpu as pltpu

F32 = jnp.float32
BF16 = jnp.bfloat16
I32 = jnp.int32

D = 1024
NB = 8
L = 2048
T = NB * L
DEPTH = 2
NS = 32
PAGE = 128
N_PAGES = 64
H_C = 8
HD = 64
W_A = 256
W_B = 256
W_C = 512
D_IN = 2824
D_IN_PAD = 2944
NG = 4
EPG = 8
NE = NG * EPG
DE = 512
D_PLE = 256
CHUNK = 128
EPS = 1e-6
NEG = -jnp.inf

T_ALL = T + NS
S_SLOTS = 2 * T_ALL
TMX = 256
NT = S_SLOTS // TMX + NE
S_PAD = NT * TMX

TM_A = 512
TQ = 512
TM_C = 512
TM_F = 512
PP = 8
NPG = N_PAGES // PP

VMEM_LIMIT = 56 * 1024 * 1024


def _cparams(sem):
    return pltpu.CompilerParams(dimension_semantics=sem, vmem_limit_bytes=VMEM_LIMIT)


def _split(a):
    hi = a.astype(BF16)
    lo = (a - hi.astype(F32)).astype(BF16)
    return hi, lo


def _dot(a, b):
    return jnp.dot(a, b, preferred_element_type=F32)


def _dot_nt(a, b):
    return lax.dot_general(a, b, (((1,), (1,)), ((), ())), preferred_element_type=F32)


def _mm(a, b, mode):
    if mode == "bf16":
        return _dot(a.astype(BF16), b.astype(BF16))
    a_hi, a_lo = _split(a)
    b_hi, b_lo = _split(b)
    return _dot(a_hi, b_hi) + _dot(a_lo, b_hi) + _dot(a_hi, b_lo)


def _mm_nt3(a, b):
    a_hi, a_lo = _split(a)
    b_hi, b_lo = _split(b)
    return _dot_nt(a_hi, b_hi) + _dot_nt(a_lo, b_hi) + _dot_nt(a_hi, b_lo)


def _rms(x, g):
    return x * lax.rsqrt(jnp.mean(x * x, axis=-1, keepdims=True) + EPS) * g


def _log_sigmoid(x):
    return jnp.minimum(x, 0.0) - jnp.log1p(jnp.exp(-jnp.abs(x)))


def _sigmoid(x):
    return 1.0 / (1.0 + jnp.exp(-x))


def _prompt_in_kernel(x_ref, gmix_ref, win_ref, convw_ref, lng_ref, lnb_ref, ws_ref, bse_ref,
                      bf_ref, gout_ref,
                      mab_ref, qbf_ref, kbf_ref, vbf_ref, k_ref, v_ref, logf_ref, c_ref, ctail_ref,
                      zbuf, ccar):
    tm = TM_A
    i = pl.program_id(0)

    @pl.when(i % (L // tm) == 0)
    def _():
        zbuf[0:8, :] = jnp.zeros((8, W_A), F32)
        ccar[...] = jnp.zeros_like(ccar)

    hb = _rms(x_ref[...], gmix_ref[...]).astype(BF16)

    def proj(lo, hi):
        return _dot(hb, win_ref[:, lo:hi])

    pa = proj(0, 3 * W_A)
    z = pa[:, W_A:2 * W_A] * pa[:, 2 * W_A:3 * W_A]
    zbuf[8:8 + tm, :] = z
    z1 = zbuf[7:7 + tm, :]
    z2 = zbuf[6:6 + tm, :]
    cw = convw_ref[...]
    y_a = pa[:, 0:W_A] * (cw[0:1, :] * z2 + cw[1:2, :] * z1 + cw[2:3, :] * z)
    tail = z[tm - 2:tm, :]
    ctail_ref[...] = tail
    zbuf[6:8, :] = tail
    gout = gout_ref[...]
    mab_ref[:, 0:W_A] = _rms(y_a, gout[:, 0:W_A]).astype(BF16)

    o = 3 * W_A
    pb = proj(o, o + 2 * W_B)
    u_b = pb[:, 0:W_B]
    v_b = pb[:, W_B:2 * W_B]
    mu = jnp.mean(v_b, axis=-1, keepdims=True)
    vc = v_b - mu
    var = jnp.mean(vc * vc, axis=-1, keepdims=True)
    vn = vc * lax.rsqrt(var + EPS) * lng_ref[...] + lnb_ref[...]
    vnb = vn.astype(BF16)
    r_i = lax.broadcasted_iota(I32, (CHUNK, CHUNK), 0)
    c_i = lax.broadcasted_iota(I32, (CHUNK, CHUNK), 1)
    lane_head = lax.broadcasted_iota(I32, (CHUNK, W_B), 1) // HD
    ws_t = [jnp.where(r_i >= c_i, ws_ref[hh], 0.0).astype(BF16) for hh in range(4)]
    parts = []
    for cidx in range(tm // CHUNK):
        vchunk = vnb[cidx * CHUNK:(cidx + 1) * CHUNK, :]
        sc = jnp.zeros((CHUNK, W_B), F32)
        for hh in range(4):
            sc = jnp.where(lane_head == hh, _dot(ws_t[hh], vchunk), sc)
        parts.append(sc + bse_ref[...])
    s = jnp.concatenate(parts, axis=0)
    y_b = u_b * s
    mab_ref[:, W_A:W_A + W_B] = _rms(y_b, gout[:, W_A:W_A + W_B]).astype(BF16)

    o = 3 * W_A + 2 * W_B
    q = proj(o, o + W_C)
    k = proj(o + W_C, o + 2 * W_C)
    v = proj(o + 2 * W_C, o + 3 * W_C)
    f = proj(o + 3 * W_C, D_IN_PAD)
    qbf_ref[...] = (q * (HD ** -0.5)).astype(BF16)
    k_ref[...] = k
    v_ref[...] = v
    kbf_ref[...] = k.astype(BF16)
    vbf_ref[...] = v.astype(BF16)
    lf = _log_sigmoid(f + bf_ref[...])
    logf_ref[...] = lf[:, 0:H_C]

    p1 = lf.astype(BF16)
    r1 = lf - p1.astype(F32)
    p2 = r1.astype(BF16)
    p3 = (r1 - p2.astype(F32)).astype(BF16)
    tr = lax.broadcasted_iota(I32, (tm, tm), 0) >= lax.broadcasted_iota(I32, (tm, tm), 1)
    trb = jnp.where(tr, 1.0, 0.0).astype(BF16)
    cs = _dot(trb, p1) + _dot(trb, p2) + _dot(trb, p3) + ccar[...]
    c_ref[...] = cs[:, 0:H_C]
    ccar[...] = cs[tm - 1:tm, :]


def _prompt_in(x, gmix, win_bf, convw, lng, lnb, ws, bse, bf_pad, gout):
    tm = TM_A
    n = T // tm
    full = lambda shape: pl.BlockSpec(shape, lambda i: (0,) * len(shape))
    tok = lambda w: pl.BlockSpec((tm, w), lambda i: (i, 0))
    return pl.pallas_call(
        _prompt_in_kernel,
        grid=(n,),
        in_specs=[tok(D), full((1, D)), full((D, D_IN_PAD)), full((3, W_A)), full((1, W_B)),
                  full((1, W_B)), full((4, CHUNK, CHUNK)), full((CHUNK, W_B)), full((1, 128)),
                  full((1, D))],
        out_specs=[tok(W_A + W_B), tok(W_C), tok(W_C), tok(W_C), tok(W_C), tok(W_C), tok(H_C), tok(H_C),
                   pl.BlockSpec((None, 2, W_A), lambda i: (i // (L // tm), 0, 0))],
        out_shape=[jax.ShapeDtypeStruct((T, W_A + W_B), BF16),
                   jax.ShapeDtypeStruct((T, W_C), BF16),
                   jax.ShapeDtypeStruct((T, W_C), BF16),
                   jax.ShapeDtypeStruct((T, W_C), BF16),
                   jax.ShapeDtypeStruct((T, W_C), F32),
                   jax.ShapeDtypeStruct((T, W_C), F32),
                   jax.ShapeDtypeStruct((T, H_C), F32),
                   jax.ShapeDtypeStruct((T, H_C), F32),
                   jax.ShapeDtypeStruct((NB, 2, W_A), F32)],
        scratch_shapes=[pltpu.VMEM((tm + 8, W_A), F32), pltpu.VMEM((1, 128), F32)],
        compiler_params=_cparams(("arbitrary",)),
        name="prompt_in",
    )(x, gmix, win_bf, convw, lng, lnb, ws, bse, bf_pad, gout)


def _prompt_attn_kernel(q_ref, k_ref, v_ref, c_ref, ct_ref, gout_ref, o_ref, yc):
    tq = TQ
    qi = pl.program_id(1)
    left = lax.broadcasted_iota(I32, (1, 128), 1) < HD
    row = lax.broadcasted_iota(I32, (tq, tq), 0)
    col = lax.broadcasted_iota(I32, (tq, tq), 1)
    zero = jnp.zeros((), BF16)

    for hp in range(H_C // 2):
        ls = slice(hp * 128, (hp + 1) * 128)
        q2 = q_ref[:, ls]
        q_a = jnp.where(left, q2, zero)
        q_b = jnp.where(left, zero, q2)
        h_a, h_b = 2 * hp, 2 * hp + 1
        cq_a = c_ref[:, h_a:h_a + 1]
        cq_b = c_ref[:, h_b:h_b + 1]

        def block(ki, carry, masked):
            m_a, l_a, m_b, l_b, acc = carry
            ks = pl.multiple_of(ki * tq, tq)
            k2 = k_ref[pl.ds(ks, tq), ls]
            v2 = v_ref[pl.ds(ks, tq), ls]

            def one(qh, cq, h, m, l):
                s = _dot_nt(qh, k2) + cq - ct_ref[h:h + 1, pl.ds(ks, tq)]
                if masked:
                    s = jnp.where(col <= row, s, NEG)
                mn = jnp.maximum(m, jnp.max(s, axis=-1, keepdims=True))
                a = jnp.exp(m - mn)
                p = jnp.exp(s - mn)
                ln = a * l + jnp.sum(p, axis=-1, keepdims=True)
                return mn, ln, a, _dot(p.astype(BF16), v2)

            m_a, l_a, a_a, pv_a = one(q_a, cq_a, h_a, m_a, l_a)
            m_b, l_b, a_b, pv_b = one(q_b, cq_b, h_b, m_b, l_b)
            acc = acc * jnp.where(left, a_a, a_b) + jnp.where(left, pv_a, pv_b)
            return m_a, l_a, m_b, l_b, acc

        init = (jnp.full((tq, 1), NEG, F32), jnp.zeros((tq, 1), F32),
                jnp.full((tq, 1), NEG, F32), jnp.zeros((tq, 1), F32),
                jnp.zeros((tq, 128), F32))
        carry = lax.fori_loop(0, qi, lambda ki, c: block(ki, c, False), init)
        m_a, l_a, m_b, l_b, acc = block(qi, carry, True)
        yc[:, ls] = acc / jnp.where(left, l_a, l_b)

    o_ref[...] = _rms(yc[...], gout_ref[:, W_A + W_B:]).astype(BF16)


def _prompt_attn(qbf, kbf, vbf, c, ct, gout):
    nq = L // TQ
    return pl.pallas_call(
        _prompt_attn_kernel,
        grid=(NB, nq),
        in_specs=[pl.BlockSpec((TQ, W_C), lambda n, qi: (n * nq + qi, 0)),
                  pl.BlockSpec((L, W_C), lambda n, qi: (n, 0)),
                  pl.BlockSpec((L, W_C), lambda n, qi: (n, 0)),
                  pl.BlockSpec((TQ, H_C), lambda n, qi: (n * nq + qi, 0)),
                  pl.BlockSpec((None, H_C, L), lambda n, qi: (n, 0, 0)),
                  pl.BlockSpec((1, D), lambda n, qi: (0, 0))],
        out_specs=pl.BlockSpec((TQ, W_C), lambda n, qi: (n * nq + qi, 0)),
        out_shape=jax.ShapeDtypeStruct((T, W_C), BF16),
        scratch_shapes=[pltpu.VMEM((TQ, W_C), F32)],
        compiler_params=_cparams(("arbitrary", "arbitrary")),
        name="prompt_attn",
    )(qbf, kbf, vbf, c, ct, gout)


def _route(logits):
    lane = lax.broadcasted_iota(I32, logits.shape, 1).astype(F32)
    big = jnp.float32(1e9)
    gl = jnp.where(lane < NG, logits, NEG)
    gmax = jnp.max(gl, axis=-1, keepdims=True)
    p_g = 1.0 / jnp.sum(jnp.exp(gl - gmax), axis=-1, keepdims=True)
    g_idx = jnp.min(jnp.where(gl == gmax, lane, big), axis=-1, keepdims=True)
    lo = NG + EPG * g_idx
    el = jnp.where((lane >= lo) & (lane < lo + EPG), logits, NEG)
    m1 = jnp.max(el, axis=-1, keepdims=True)
    i1 = jnp.min(jnp.where(el == m1, lane, big), axis=-1, keepdims=True)
    zsum = jnp.sum(jnp.exp(el - m1), axis=-1, keepdims=True)
    el2 = jnp.where(lane == i1, NEG, el)
    m2 = jnp.max(el2, axis=-1, keepdims=True)
    i2 = jnp.min(jnp.where(el2 == m2, lane, big), axis=-1, keepdims=True)
    p1 = 1.0 / zsum
    p2 = jnp.exp(m2 - m1) / zsum
    den = p1 + p2
    w1 = p_g * p1 / den
    w2 = p_g * p2 / den
    info = jnp.where(lane == 0, i1 - NG,
                     jnp.where(lane == 1, i2 - NG,
                               jnp.where(lane == 2, w1, jnp.where(lane == 3, w2, 0.0))))
    return info


def _out_route_kernel(mab_ref, mc_ref, x_ref, wo_ref, gffn_ref, wr_ref, br_ref,
                      xmid_ref, h2_ref, info_ref, *, mode):
    upd = _mm(mab_ref[...], wo_ref[0:W_A + W_B, :], mode) + _mm(mc_ref[...], wo_ref[W_A + W_B:, :], mode)
    xm = x_ref[...] + upd
    xmid_ref[...] = xm
    h2 = _rms(xm, gffn_ref[...])
    h2_ref[...] = h2.astype(BF16)
    logits = _mm(h2, wr_ref[...], mode) + br_ref[...]
    info_ref[...] = _route(logits)


def _out_route(mab, mc, x, wo, gffn, wr, br, *, mode, tm):
    n = x.shape[0] // tm
    full = lambda shape: pl.BlockSpec(shape, lambda i: (0,) * len(shape))
    tok = lambda w: pl.BlockSpec((tm, w), lambda i: (i, 0))
    return pl.pallas_call(
        functools.partial(_out_route_kernel, mode=mode),
        grid=(n,),
        in_specs=[tok(W_A + W_B), tok(W_C), tok(D), full((D, D)), full((1, D)), full((D, 128)),
                  full((1, 128))],
        out_specs=[tok(D), tok(D), tok(128)],
        out_shape=[jax.ShapeDtypeStruct((x.shape[0], D), F32),
                   jax.ShapeDtypeStruct((x.shape[0], D), BF16),
                   jax.ShapeDtypeStruct((x.shape[0], 128), F32)],
        compiler_params=_cparams(("arbitrary",)),
        name="out_route_" + mode,
    )(mab, mc, x, wo, gffn, wr, br)


def _expert_kernel(te_ref, tc_ref, xs_ref, wg_ref, wu_ref, wd_ref, y_ref, wg_b, wu_b, wd_b):
    t = pl.program_id(0)
    prev = te_ref[jnp.maximum(t - 1, 0)]

    @pl.when((t == 0) | (te_ref[t] != prev))
    def _():
        wg_b[...] = wg_ref[...].astype(BF16)
        wu_b[...] = wu_ref[...].astype(BF16)
        wd_b[...] = wd_ref[...].astype(BF16)

    @pl.when(tc_ref[t] > 0)
    def _():
        xb = xs_ref[...]
        a = _dot(xb, wg_b[...])
        b = _dot(xb, wu_b[...])
        act = (a * _sigmoid(a) * b).astype(BF16)
        y_ref[...] = _dot(act, wd_b[...])

    @pl.when(tc_ref[t] == 0)
    def _():
        y_ref[...] = jnp.zeros_like(y_ref)


def _experts(tile_e, tile_cnt, xs_sorted, wg, wu, wd, layer):
    grid_spec = pltpu.PrefetchScalarGridSpec(
        num_scalar_prefetch=2,
        grid=(NT,),
        in_specs=[pl.BlockSpec((TMX, D), lambda t, te, tc: (t, 0)),
                  pl.BlockSpec((None, D, DE), lambda t, te, tc: (layer * NE + te[t], 0, 0)),
                  pl.BlockSpec((None, D, DE), lambda t, te, tc: (layer * NE + te[t], 0, 0)),
                  pl.BlockSpec((None, DE, D), lambda t, te, tc: (layer * NE + te[t], 0, 0))],
        out_specs=pl.BlockSpec((TMX, D), lambda t, te, tc: (t, 0)),
        scratch_shapes=[pltpu.VMEM((D, DE), BF16), pltpu.VMEM((D, DE), BF16), pltpu.VMEM((DE, D), BF16)],
    )
    return pl.pallas_call(
        _expert_kernel,
        grid_spec=grid_spec,
        out_shape=jax.ShapeDtypeStruct((S_PAD, D), F32),
        compiler_params=_cparams(("arbitrary",)),
        name="experts",
    )(tile_e, tile_cnt, xs_sorted, wg, wu, wd)


def _combine_ple_kernel(xmid_ref, yg_ref, info_ref, p_ref, wple_ref, wpg_ref, gple_ref, gfin_ref,
                        x_ref, y_ref, *, mode):
    info = info_ref[...]
    w1 = info[:, 2:3]
    w2 = info[:, 3:4]
    x2 = xmid_ref[...] + w1 * yg_ref[:, 0:D] + w2 * yg_ref[:, D:2 * D]
    gate = _sigmoid(_mm(_rms(x2, gple_ref[...]), wpg_ref[...], mode))
    x3 = x2 + _mm(p_ref[...], wple_ref[...], mode) * gate
    x_ref[...] = x3
    y_ref[...] = _rms(x3, gfin_ref[...])


def _combine_ple(xmid, yg, info, p, wple, wpg, gple, gfin, *, mode, tm):
    n_tok = xmid.shape[0]
    full = lambda shape: pl.BlockSpec(shape, lambda i: (0,) * len(shape))
    tok = lambda w: pl.BlockSpec((tm, w), lambda i: (i, 0))
    return pl.pallas_call(
        functools.partial(_combine_ple_kernel, mode=mode),
        grid=(n_tok // tm,),
        in_specs=[tok(D), tok(2 * D), tok(128), tok(D_PLE), full((D_PLE, D)), full((D, D)),
                  full((1, D)), full((1, D))],
        out_specs=[tok(D), tok(D)],
        out_shape=[jax.ShapeDtypeStruct((n_tok, D), F32), jax.ShapeDtypeStruct((n_tok, D), F32)],
        compiler_params=_cparams(("arbitrary",)),
        name="combine_ple_" + mode,
    )(xmid, yg, info, p, wple, wpg, gple, gfin)


def _sample_in_kernel(x_ref, gmix_ref, win_ref, convw_ref, s0_ref, s1_ref, lng_ref, lnb_ref,
                      ws0_ref, bs0_ref, bf_ref, gout_ref,
                      mab_ref, q_ref, k_ref, v_ref, logf_ref, z_ref, vn_ref):
    h = _rms(x_ref[...], gmix_ref[...])
    h_hi, h_lo = _split(h)

    def proj(lo, hi):
        w_hi, w_lo = _split(win_ref[:, lo:hi])
        return _dot(h_hi, w_hi) + _dot(h_lo, w_hi) + _dot(h_hi, w_lo)

    pa = proj(0, 3 * W_A)
    z = pa[:, W_A:2 * W_A] * pa[:, 2 * W_A:3 * W_A]
    cw = convw_ref[...]
    y_a = pa[:, 0:W_A] * (cw[0:1, :] * s0_ref[...] + cw[1:2, :] * s1_ref[...] + cw[2:3, :] * z)
    z_ref[...] = z
    gout = gout_ref[...]
    mab_ref[:, 0:W_A] = _rms(y_a, gout[:, 0:W_A])

    o = 3 * W_A
    pb = proj(o, o + 2 * W_B)
    v_b = pb[:, W_B:2 * W_B]
    mu = jnp.mean(v_b, axis=-1, keepdims=True)
    vc = v_b - mu
    var = jnp.mean(vc * vc, axis=-1, keepdims=True)
    vn = vc * lax.rsqrt(var + EPS) * lng_ref[...] + lnb_ref[...]
    vn_ref[...] = vn
    y_b = pb[:, 0:W_B] * (ws0_ref[...] * vn + bs0_ref[...])
    mab_ref[:, W_A:W_A + W_B] = _rms(y_b, gout[:, W_A:W_A + W_B])

    o = 3 * W_A + 2 * W_B
    q_ref[...] = proj(o, o + W_C) * (HD ** -0.5)
    k_ref[...] = proj(o + W_C, o + 2 * W_C)
    v_ref[...] = proj(o + 2 * W_C, o + 3 * W_C)
    f = proj(o + 3 * W_C, D_IN_PAD)
    logf_ref[...] = _log_sigmoid(f + bf_ref[...])


def _sample_in(x, gmix, win_pad, convw, s0, s1, lng, lnb, ws0, bs0, bf_pad, gout):
    shapes = [(NS, W_A + W_B), (NS, W_C), (NS, W_C), (NS, W_C), (NS, 128), (NS, W_A), (NS, W_B)]
    return pl.pallas_call(
        _sample_in_kernel,
        out_shape=[jax.ShapeDtypeStruct(s, F32) for s in shapes],
        compiler_params=pltpu.CompilerParams(vmem_limit_bytes=VMEM_LIMIT),
        name="sample_in",
    )(x, gmix, win_pad, convw, s0, s1, lng, lnb, ws0, bs0, bf_pad, gout)


def _sample_attn_kernel(pt_ref, q_ref, kn_ref, vn_ref, lfn_ref, gout_ref, *rest):
    k_refs = rest[0:PP]
    v_refs = rest[PP:2 * PP]
    lf_refs = rest[2 * PP:3 * PP]
    o_ref = rest[3 * PP]
    m_s, l_s, acc_s, tot_s = rest[3 * PP + 1:]
    j = pl.program_id(1)

    @pl.when(j == 0)
    def _():
        m_s[...] = jnp.full_like(m_s, NEG)
        l_s[...] = jnp.zeros_like(l_s)
        acc_s[...] = jnp.zeros_like(acc_s)
        tot_s[...] = jnp.zeros_like(tot_s)

    head_of_lane = lax.broadcasted_iota(I32, (H_C, W_C), 1) // HD
    diag = head_of_lane == lax.broadcasted_iota(I32, (H_C, W_C), 0)
    qbd = jnp.where(diag, q_ref[...], 0.0)
    q_hi, q_lo = _split(qbd)
    cn = lfn_ref[...]
    lane = lax.broadcasted_iota(I32, (H_C, PAGE), 1)

    m = m_s[...]
    l = l_s[...]
    acc = acc_s[...]
    tot = tot_s[...]
    for pp in reversed(range(PP)):
        lf = lf_refs[pp][...]
        inc = lf
        sh = 1
        while sh < PAGE:
            inc = inc + jnp.where(lane >= sh, pltpu.roll(inc, sh, axis=1), 0.0)
            sh *= 2
        page_tot = inc[:, PAGE - 1:PAGE]
        suffix = (page_tot - inc) + tot
        tot = tot + page_tot
        k_hi, k_lo = _split(k_refs[pp][...])
        s = _dot_nt(q_hi, k_hi) + _dot_nt(q_lo, k_hi) + _dot_nt(q_hi, k_lo) + cn + suffix
        mn = jnp.maximum(m, jnp.max(s, axis=-1, keepdims=True))
        a = jnp.exp(m - mn)
        p = jnp.exp(s - mn)
        l = a * l + jnp.sum(p, axis=-1, keepdims=True)
        acc = a * acc + _mm(p, v_refs[pp][...], "x3")
        m = mn
    m_s[...] = m
    l_s[...] = l
    acc_s[...] = acc
    tot_s[...] = tot

    @pl.when(j == NPG - 1)
    def _():
        s_new = jnp.sum(qbd * kn_ref[...], axis=-1, keepdims=True)
        mn = jnp.maximum(m, s_new)
        a = jnp.exp(m - mn)
        pn = jnp.exp(s_new - mn)
        lt = a * l + pn
        o8 = (a * acc + pn * vn_ref[...]) / lt
        o = jnp.sum(jnp.where(diag, o8, 0.0), axis=0, keepdims=True)
        o_ref[...] = _rms(o, gout_ref[:, W_A + W_B:])


def _sample_attn(pt_flat, q3, kn3, vn3, lfn3, gout, ck, cv, clft, layer):
    def page_map(pp):
        return lambda n, j, pt: (layer, pt[n * N_PAGES + (NPG - 1 - j) * PP + pp], 0, 0)

    tok3 = lambda w: pl.BlockSpec((None, 1, w), lambda n, j, pt: (n, 0, 0))
    in_specs = [tok3(W_C), tok3(W_C), tok3(W_C),
                pl.BlockSpec((None, H_C, 1), lambda n, j, pt: (n, 0, 0)),
                pl.BlockSpec((1, D), lambda n, j, pt: (0, 0))]
    in_specs += [pl.BlockSpec((None, None, PAGE, W_C), page_map(pp)) for pp in range(PP)]
    in_specs += [pl.BlockSpec((None, None, PAGE, W_C), page_map(pp)) for pp in range(PP)]
    in_specs += [pl.BlockSpec((None, None, H_C, PAGE), page_map(pp)) for pp in range(PP)]
    grid_spec = pltpu.PrefetchScalarGridSpec(
        num_scalar_prefetch=1,
        grid=(NS, NPG),
        in_specs=in_specs,
        out_specs=tok3(W_C),
        scratch_shapes=[pltpu.VMEM((H_C, 1), F32), pltpu.VMEM((H_C, 1), F32),
                        pltpu.VMEM((H_C, W_C), F32), pltpu.VMEM((H_C, 1), F32)],
    )
    return pl.pallas_call(
        _sample_attn_kernel,
        grid_spec=grid_spec,
        out_shape=jax.ShapeDtypeStruct((NS, 1, W_C), F32),
        compiler_params=_cparams(("arbitrary", "arbitrary")),
        name="sample_attn",
    )(pt_flat, q3, kn3, vn3, lfn3, gout, *([ck] * PP), *([cv] * PP), *([clft] * PP))


def _schedule(eid_all):
    flat = eid_all.reshape(-1)
    order = jnp.argsort(flat).astype(I32)
    sorted_e = flat[order]
    off = jnp.searchsorted(sorted_e, jnp.arange(NE + 1, dtype=I32)).astype(I32)
    cnt = off[1:] - off[:-1]
    ntile = (cnt + TMX - 1) // TMX
    tcum = jnp.cumsum(ntile).astype(I32)
    tstart = tcum - ntile
    t = jnp.arange(NT, dtype=I32)
    tile_e = jnp.minimum(jnp.searchsorted(tcum, t, side="right").astype(I32), NE - 1)
    within = t - tstart[tile_e]
    tile_cnt = jnp.clip(cnt[tile_e] - within * TMX, 0, TMX).astype(I32)
    tile_start = off[tile_e] + within * TMX
    sidx = jnp.minimum(tile_start[:, None] + jnp.arange(TMX, dtype=I32)[None, :], S_SLOTS - 1)
    src_tok = (order[sidx.reshape(-1)] // 2).astype(I32)
    padpos = tstart[sorted_e] * TMX + (jnp.arange(S_SLOTS, dtype=I32) - off[sorted_e])
    pos = jnp.zeros((S_SLOTS,), I32).at[order].set(padpos)
    return tile_e, tile_cnt, src_tok, pos


def kernel(x_prompt, x_sample, p_prompt, p_sample, cache_k, cache_v, cache_logf, state_conv, page_table, g_mix, w_in, conv_w, ln_g, ln_b, w_s, b_s, b_f, g_out, w_o, g_ffn, w_grp, b_grp, w_rt, b_rt, w_gate, w_up, w_down, g_ple, w_ple, w_ple_gate, g_final):
    n_pool = cache_k.shape[1]
    xp = x_prompt.reshape(T, D)
    xs = x_sample.reshape(NS, D)
    pt_flat = page_table.reshape(-1).astype(I32)
    ck = cache_k.reshape(DEPTH, n_pool, PAGE, W_C)
    cv = cache_v.reshape(DEPTH, n_pool, PAGE, W_C)
    clft = jnp.swapaxes(cache_logf, 2, 3)
    wg_all = w_gate.reshape(DEPTH * NE, D, DE)
    wu_all = w_up.reshape(DEPTH * NE, D, DE)
    wd_all = w_down.reshape(DEPTH * NE, DE, D)
    gfin = g_final.reshape(1, D)

    outs = {k: [] for k in ("kp", "vp", "lfp", "cvp", "ks", "vs", "lfs", "cvs", "chv")}
    yp = ys = None
    for i in range(DEPTH):
        gmix = g_mix[i].reshape(1, D)
        gout = g_out[i].reshape(1, D)
        gffn = g_ffn[i].reshape(1, D)
        gple = g_ple[i].reshape(1, D)
        win_pad = jnp.pad(w_in[i], ((0, 0), (0, D_IN_PAD - D_IN)))
        win_bf = win_pad.astype(BF16)
        bf_pad = jnp.pad(b_f[i], (0, 128 - H_C)).reshape(1, 128)
        lng = ln_g[i].reshape(1, W_B)
        lnb = ln_b[i].reshape(1, W_B)
        bse = jnp.repeat(b_s[i].T, HD, axis=1)
        ws0 = jnp.repeat(w_s[i][:, 0, 0], HD).reshape(1, W_B)
        bs0 = jnp.repeat(b_s[i][:, 0], HD).reshape(1, W_B)
        wr = jnp.pad(jnp.concatenate([w_grp[i], w_rt[i].reshape(D, NE)], axis=1), ((0, 0), (0, 128 - NG - NE)))
        br = jnp.pad(jnp.concatenate([b_grp[i], b_rt[i].reshape(NE)]), (0, 128 - NG - NE)).reshape(1, 128)

        mab, qbf, kbf, vbf, k_p, v_p, lf_p, c_p, ctail = _prompt_in(
            xp, gmix, win_bf, conv_w[i], lng, lnb, w_s[i], bse, bf_pad, gout)
        ct = jnp.swapaxes(c_p.reshape(NB, L, H_C), 1, 2)
        mc = _prompt_attn(qbf, kbf, vbf, c_p, ct, gout)
        xmid_p, h2_p, info_p = _out_route(mab, mc, xp, w_o[i].astype(BF16), gffn, wr.astype(BF16), br,
                                          mode="bf16", tm=TM_C)

        mab_s, q_s, k_s, v_s, lf_s128, z_s, vn_s = _sample_in(
            xs, gmix, win_pad, conv_w[i], state_conv[i, :, 0], state_conv[i, :, 1], lng, lnb, ws0, bs0,
            bf_pad, gout)
        lf_s = lf_s128[:, :H_C]
        mc_s = _sample_attn(pt_flat, q_s.reshape(NS, 1, W_C), k_s.reshape(NS, 1, W_C),
                            v_s.reshape(NS, 1, W_C), lf_s.reshape(NS, H_C, 1), gout, ck, cv, clft, i)
        xmid_s, h2_s, info_s = _out_route(mab_s, mc_s.reshape(NS, W_C), xs, w_o[i], gffn, wr, br,
                                          mode="x3", tm=NS)

        eid_all = jnp.concatenate([info_p[:, :2], info_s[:, :2]], axis=0).astype(I32)
        tile_e, tile_cnt, src_tok, pos = _schedule(eid_all)
        h2_all = jnp.concatenate([h2_p, h2_s], axis=0)
        xs_sorted = jnp.take(h2_all, src_tok, axis=0)
        y_sorted = _experts(tile_e, tile_cnt, xs_sorted, wg_all, wu_all, wd_all, i)
        yg = jnp.take(y_sorted, pos, axis=0).reshape(T_ALL, 2 * D)

        xp, yp = _combine_ple(xmid_p, yg[:T], info_p, p_prompt[i].reshape(T, D_PLE), w_ple[i].astype(BF16),
                              w_ple_gate[i].astype(BF16), gple, gfin, mode="bf16", tm=TM_F)
        xs, ys = _combine_ple(xmid_s, yg[T:], info_s, p_sample[i].reshape(NS, D_PLE), w_ple[i],
                              w_ple_gate[i], gple, gfin, mode="x3", tm=NS)

        outs["kp"].append(k_p.reshape(NB, L, H_C, HD))
        outs["vp"].append(v_p.reshape(NB, L, H_C, HD))
        outs["lfp"].append(lf_p.reshape(NB, L, H_C))
        outs["cvp"].append(ctail)
        outs["ks"].append(k_s.reshape(NS, 1, H_C, HD))
        outs["vs"].append(v_s.reshape(NS, 1, H_C, HD))
        outs["lfs"].append(lf_s.reshape(NS, 1, H_C))
        outs["cvs"].append(jnp.stack([state_conv[i, :, 1], z_s], axis=1))
        outs["chv"].append(vn_s.reshape(NS, 1, W_B))

    st = lambda k: jnp.stack(outs[k])
    return (yp.reshape(NB, L, D), ys.reshape(NS, 1, D), st("kp"), st("vp"), st("lfp"), st("cvp"),
            st("ks"), st("vs"), st("lfs"), st("cvs"), st("chv"))
```

```python
import functools

import jax
import jax.numpy as jnp
from jax import lax
from jax.experimental import pallas as pl
from jax.experimental.pallas import tpu as pltpu

F32 = jnp.float32
BF16 = jnp.bfloat16
I32 = jnp.int32
U32 = jnp.uint32

D = 1024
NB = 8
L = 2048
T = NB * L
DEPTH = 2
NS = 32
PAGE = 128
N_PAGES = 64
H_C = 8
HD = 64
W_A = 256
W_B = 256
W_C = 512
D_IN = 2824
D_IN_PAD = 2944
NG = 4
EPG = 8
NE = NG * EPG
DE = 512
D_PLE = 256
CHUNK = 128
EPS = 1e-6
NEG = -jnp.inf

T_ALL = T + NS
S_SLOTS = 2 * T_ALL
TMX = 256
NT = S_SLOTS // TMX + NE
S_PAD = NT * TMX

TM_A = 512
TQ = 512
TM_C = 512
TM_F = 512
PP = 8
NPG = N_PAGES // PP

VMEM_LIMIT = 56 * 1024 * 1024


def _cparams(sem):
    return pltpu.CompilerParams(dimension_semantics=sem, vmem_limit_bytes=VMEM_LIMIT)


def _split(a):
    hi = a.astype(BF16)
    lo = (a - hi.astype(F32)).astype(BF16)
    return hi, lo


def _dot(a, b):
    return jnp.dot(a, b, preferred_element_type=F32)


def _dot_nt(a, b):
    return lax.dot_general(a, b, (((1,), (1,)), ((), ())), preferred_element_type=F32)


def _mm(a, b, mode):
    if mode == "bf16":
        return _dot(a.astype(BF16), b.astype(BF16))
    a_hi, a_lo = _split(a)
    b_hi, b_lo = _split(b)
    return _dot(a_hi, b_hi) + _dot(a_lo, b_hi) + _dot(a_hi, b_lo)


def _mm_nt3(a, b):
    a_hi, a_lo = _split(a)
    b_hi, b_lo = _split(b)
    return _dot_nt(a_hi, b_hi) + _dot_nt(a_lo, b_hi) + _dot_nt(a_hi, b_lo)


def _rms(x, g):
    return x * lax.rsqrt(jnp.mean(x * x, axis=-1, keepdims=True) + EPS) * g


def _log_sigmoid(x):
    return jnp.minimum(x, 0.0) - jnp.log1p(jnp.exp(-jnp.abs(x)))


def _sigmoid(x):
    return 1.0 / (1.0 + jnp.exp(-x))


def _prompt_in_kernel(x_ref, gmix_ref, win_ref, convw_ref, lng_ref, lnb_ref, ws_ref, bse_ref,
                      bf_ref, gout_ref,
                      mab_ref, qbf_ref, kbf_ref, vbf_ref, k_ref, v_ref, logf_ref, c_ref, ctail_ref,
                      zbuf, ccar):
    tm = TM_A
    i = pl.program_id(0)

    @pl.when(i % (L // tm) == 0)
    def _():
        zbuf[0:8, :] = jnp.zeros((8, W_A), F32)
        ccar[...] = jnp.zeros_like(ccar)

    hb = _rms(x_ref[...], gmix_ref[...]).astype(BF16)

    def proj(lo, hi):
        return _dot(hb, win_ref[:, lo:hi])

    pa = proj(0, 3 * W_A)
    z = pa[:, W_A:2 * W_A] * pa[:, 2 * W_A:3 * W_A]
    zbuf[8:8 + tm, :] = z
    z1 = zbuf[7:7 + tm, :]
    z2 = zbuf[6:6 + tm, :]
    cw = convw_ref[...]
    y_a = pa[:, 0:W_A] * (cw[0:1, :] * z2 + cw[1:2, :] * z1 + cw[2:3, :] * z)
    tail = z[tm - 2:tm, :]
    ctail_ref[...] = tail
    zbuf[6:8, :] = tail
    gout = gout_ref[...]
    mab_ref[:, 0:W_A] = _rms(y_a, gout[:, 0:W_A]).astype(BF16)

    o = 3 * W_A
    pb = proj(o, o + 2 * W_B)
    u_b = pb[:, 0:W_B]
    v_b = pb[:, W_B:2 * W_B]
    mu = jnp.mean(v_b, axis=-1, keepdims=True)
    vc = v_b - mu
    var = jnp.mean(vc * vc, axis=-1, keepdims=True)
    vn = vc * lax.rsqrt(var + EPS) * lng_ref[...] + lnb_ref[...]
    vnb = vn.astype(BF16)
    r_i = lax.broadcasted_iota(I32, (CHUNK, CHUNK), 0)
    c_i = lax.broadcasted_iota(I32, (CHUNK, CHUNK), 1)
    lane_head = lax.broadcasted_iota(I32, (CHUNK, W_B), 1) // HD
    ws_t = [jnp.where(r_i >= c_i, ws_ref[hh], 0.0).astype(BF16) for hh in range(4)]
    parts = []
    for cidx in range(tm // CHUNK):
        vchunk = vnb[cidx * CHUNK:(cidx + 1) * CHUNK, :]
        sc = jnp.zeros((CHUNK, W_B), F32)
        for hh in range(4):
            sc = jnp.where(lane_head == hh, _dot(ws_t[hh], vchunk), sc)
        parts.append(sc + bse_ref[...])
    s = jnp.concatenate(parts, axis=0)
    y_b = u_b * s
    mab_ref[:, W_A:W_A + W_B] = _rms(y_b, gout[:, W_A:W_A + W_B]).astype(BF16)

    o = 3 * W_A + 2 * W_B
    q = proj(o, o + W_C)
    k = proj(o + W_C, o + 2 * W_C)
    v = proj(o + 2 * W_C, o + 3 * W_C)
    f = proj(o + 3 * W_C, D_IN_PAD)
    qbf_ref[...] = (q * (HD ** -0.5)).astype(BF16)
    k_ref[...] = k
    v_ref[...] = v
    kbf_ref[...] = k.astype(BF16)
    vbf_ref[...] = v.astype(BF16)
    lf = _log_sigmoid(f + bf_ref[...])
    logf_ref[...] = lf[:, 0:H_C]

    p1 = lf.astype(BF16)
    r1 = lf - p1.astype(F32)
    p2 = r1.astype(BF16)
    p3 = (r1 - p2.astype(F32)).astype(BF16)
    tr = lax.broadcasted_iota(I32, (tm, tm), 0) >= lax.broadcasted_iota(I32, (tm, tm), 1)
    trb = jnp.where(tr, 1.0, 0.0).astype(BF16)
    cs = _dot(trb, p1) + _dot(trb, p2) + _dot(trb, p3) + ccar[...]
    c_ref[...] = cs[:, 0:H_C]
    ccar[...] = cs[tm - 1:tm, :]


def _prompt_in(x, gmix, win_bf, convw, lng, lnb, ws, bse, bf_pad, gout):
    tm = TM_A
    n = T // tm
    full = lambda shape: pl.BlockSpec(shape, lambda i: (0,) * len(shape))
    tok = lambda w: pl.BlockSpec((tm, w), lambda i: (i, 0))
    return pl.pallas_call(
        _prompt_in_kernel,
        grid=(n,),
        in_specs=[tok(D), full((1, D)), full((D, D_IN_PAD)), full((3, W_A)), full((1, W_B)),
                  full((1, W_B)), full((4, CHUNK, CHUNK)), full((CHUNK, W_B)), full((1, 128)),
                  full((1, D))],
        out_specs=[tok(W_A + W_B), tok(W_C), tok(W_C), tok(W_C), tok(W_C), tok(W_C), tok(H_C), tok(H_C),
                   pl.BlockSpec((None, 2, W_A), lambda i: (i // (L // tm), 0, 0))],
        out_shape=[jax.ShapeDtypeStruct((T, W_A + W_B), BF16),
                   jax.ShapeDtypeStruct((T, W_C), BF16),
                   jax.ShapeDtypeStruct((T, W_C), BF16),
                   jax.ShapeDtypeStruct((T, W_C), BF16),
                   jax.ShapeDtypeStruct((T, W_C), F32),
                   jax.ShapeDtypeStruct((T, W_C), F32),
                   jax.ShapeDtypeStruct((T, H_C), F32),
                   jax.ShapeDtypeStruct((T, H_C), F32),
                   jax.ShapeDtypeStruct((NB, 2, W_A), F32)],
        scratch_shapes=[pltpu.VMEM((tm + 8, W_A), F32), pltpu.VMEM((1, 128), F32)],
        compiler_params=_cparams(("arbitrary",)),
        name="prompt_in",
    )(x, gmix, win_bf, convw, lng, lnb, ws, bse, bf_pad, gout)


def _prompt_attn_kernel(q_ref, k_ref, v_ref, c_ref, ct_ref, gout_ref, o_ref, yc):
    tq = TQ
    qi = pl.program_id(1)
    left = lax.broadcasted_iota(I32, (1, 128), 1) < HD
    row = lax.broadcasted_iota(I32, (tq, tq), 0)
    col = lax.broadcasted_iota(I32, (tq, tq), 1)
    zero = jnp.zeros((), BF16)

    for hp in range(H_C // 2):
        ls = slice(hp * 128, (hp + 1) * 128)
        q2 = q_ref[:, ls]
        q_a = jnp.where(left, q2, zero)
        q_b = jnp.where(left, zero, q2)
        h_a, h_b = 2 * hp, 2 * hp + 1
        cq_a = c_ref[:, h_a:h_a + 1]
        cq_b = c_ref[:, h_b:h_b + 1]

        def block(ki, carry, masked):
            m_a, l_a, m_b, l_b, acc = carry
            ks = pl.multiple_of(ki * tq, tq)
            k2 = k_ref[pl.ds(ks, tq), ls]
            v2 = v_ref[pl.ds(ks, tq), ls]

            def one(qh, cq, h, m, l):
                s = _dot_nt(qh, k2) + cq - ct_ref[h:h + 1, pl.ds(ks, tq)]
                if masked:
                    s = jnp.where(col <= row, s, NEG)
                mn = jnp.maximum(m, jnp.max(s, axis=-1, keepdims=True))
                a = jnp.exp(m - mn)
                p = jnp.exp(s - mn)
                ln = a * l + jnp.sum(p, axis=-1, keepdims=True)
                return mn, ln, a, _dot(p.astype(BF16), v2)

            m_a, l_a, a_a, pv_a = one(q_a, cq_a, h_a, m_a, l_a)
            m_b, l_b, a_b, pv_b = one(q_b, cq_b, h_b, m_b, l_b)
            acc = acc * jnp.where(left, a_a, a_b) + jnp.where(left, pv_a, pv_b)
            return m_a, l_a, m_b, l_b, acc

        init = (jnp.full((tq, 1), NEG, F32), jnp.zeros((tq, 1), F32),
                jnp.full((tq, 1), NEG, F32), jnp.zeros((tq, 1), F32),
                jnp.zeros((tq, 128), F32))
        carry = lax.fori_loop(0, qi, lambda ki, c: block(ki, c, False), init)
        m_a, l_a, m_b, l_b, acc = block(qi, carry, True)
        yc[:, ls] = acc / jnp.where(left, l_a, l_b)

    o_ref[...] = _rms(yc[...], gout_ref[:, W_A + W_B:]).astype(BF16)


def _prompt_attn(qbf, kbf, vbf, c, ct, gout):
    nq = L // TQ
    return pl.pallas_call(
        _prompt_attn_kernel,
        grid=(NB, nq),
        in_specs=[pl.BlockSpec((TQ, W_C), lambda n, qi: (n * nq + qi, 0)),
                  pl.BlockSpec((L, W_C), lambda n, qi: (n, 0)),
                  pl.BlockSpec((L, W_C), lambda n, qi: (n, 0)),
                  pl.BlockSpec((TQ, H_C), lambda n, qi: (n * nq + qi, 0)),
                  pl.BlockSpec((None, H_C, L), lambda n, qi: (n, 0, 0)),
                  pl.BlockSpec((1, D), lambda n, qi: (0, 0))],
        out_specs=pl.BlockSpec((TQ, W_C), lambda n, qi: (n * nq + qi, 0)),
        out_shape=jax.ShapeDtypeStruct((T, W_C), BF16),
        scratch_shapes=[pltpu.VMEM((TQ, W_C), F32)],
        compiler_params=_cparams(("arbitrary", "arbitrary")),
        name="prompt_attn",
    )(qbf, kbf, vbf, c, ct, gout)


def _route(logits):
    lane = lax.broadcasted_iota(I32, logits.shape, 1).astype(F32)
    big = jnp.float32(1e9)
    gl = jnp.where(lane < NG, logits, NEG)
    gmax = jnp.max(gl, axis=-1, keepdims=True)
    p_g = 1.0 / jnp.sum(jnp.exp(gl - gmax), axis=-1, keepdims=True)
    g_idx = jnp.min(jnp.where(gl == gmax, lane, big), axis=-1, keepdims=True)
    lo = NG + EPG * g_idx
    el = jnp.where((lane >= lo) & (lane < lo + EPG), logits, NEG)
    m1 = jnp.max(el, axis=-1, keepdims=True)
    i1 = jnp.min(jnp.where(el == m1, lane, big), axis=-1, keepdims=True)
    zsum = jnp.sum(jnp.exp(el - m1), axis=-1, keepdims=True)
    el2 = jnp.where(lane == i1, NEG, el)
    m2 = jnp.max(el2, axis=-1, keepdims=True)
    i2 = jnp.min(jnp.where(el2 == m2, lane, big), axis=-1, keepdims=True)
    p1 = 1.0 / zsum
    p2 = jnp.exp(m2 - m1) / zsum
    den = p1 + p2
    w1 = p_g * p1 / den
    w2 = p_g * p2 / den
    e1 = i1 - NG
    e2 = i2 - NG
    info = jnp.where(lane == 0, e1,
                     jnp.where(lane == 1, e2,
                               jnp.where(lane == 2, w1, jnp.where(lane == 3, w2, 0.0))))
    return info, e1, e2


def _pack_bf16_pairs(x):
    u = lax.bitcast_convert_type(x.astype(BF16).astype(F32), U32)
    return (u[:, D // 2:] & jnp.uint32(0xFFFF0000)) | (u[:, :D // 2] >> 16)


def _unpack_bf16_pairs(p):
    lo = lax.bitcast_convert_type(p << 16, F32).astype(BF16)
    hi = lax.bitcast_convert_type(p & jnp.uint32(0xFFFF0000), F32).astype(BF16)
    return lo, hi


def _out_route_kernel(mab_ref, mc_ref, x_ref, wo_ref, gffn_ref, wr_ref, br_ref, cin_ref,
                      xmid_ref, h2_ref, info_ref, cnt_ref, carry, *, mode, tm):
    @pl.when(pl.program_id(0) == 0)
    def _():
        carry[...] = cin_ref[...]

    upd = _mm(mab_ref[...], wo_ref[0:W_A + W_B, :], mode) + _mm(mc_ref[...], wo_ref[W_A + W_B:, :], mode)
    xm = x_ref[...] + upd
    xmid_ref[...] = xm
    h2 = _rms(xm, gffn_ref[...])
    packed = _pack_bf16_pairs(h2)
    for c in range(4):
        h2_ref[:, c, :] = packed[:, c * 128:(c + 1) * 128]
    logits = _mm(h2, wr_ref[...], mode) + br_ref[...]
    info, e1, e2 = _route(logits)

    lane = lax.broadcasted_iota(I32, (tm, 128), 1).astype(F32)
    hit1 = lane == e1
    hit2 = lane == e2
    both = jnp.where(hit1, 1.0, 0.0) + jnp.where(hit2, 1.0, 0.0)
    earlier = lax.broadcasted_iota(I32, (tm, tm), 0) > lax.broadcasted_iota(I32, (tm, tm), 1)
    before = _dot(jnp.where(earlier, 1.0, 0.0).astype(BF16), both.astype(BF16)) + carry[...]
    r1 = jnp.sum(jnp.where(hit1, before, 0.0), axis=-1, keepdims=True)
    r2 = jnp.sum(jnp.where(hit2, before, 0.0), axis=-1, keepdims=True)
    info_ref[...] = jnp.where(lane == 4, r1, jnp.where(lane == 5, r2, info))
    total = carry[...] + jnp.sum(both, axis=0, keepdims=True)
    carry[...] = total
    cnt_ref[...] = total


def _out_route(mab, mc, x, wo, gffn, wr, br, cnt_in, *, mode, tm):
    n_tok = x.shape[0]
    full = lambda shape: pl.BlockSpec(shape, lambda i: (0,) * len(shape))
    tok = lambda w: pl.BlockSpec((tm, w), lambda i: (i, 0))
    return pl.pallas_call(
        functools.partial(_out_route_kernel, mode=mode, tm=tm),
        grid=(n_tok // tm,),
        in_specs=[tok(W_A + W_B), tok(W_C), tok(D), full((D, D)), full((1, D)), full((D, 128)),
                  full((1, 128)), full((1, 128))],
        out_specs=[tok(D), pl.BlockSpec((tm, 4, 128), lambda i: (i, 0, 0)), tok(128), full((1, 128))],
        out_shape=[jax.ShapeDtypeStruct((n_tok, D), F32),
                   jax.ShapeDtypeStruct((n_tok, 4, 128), U32),
                   jax.ShapeDtypeStruct((n_tok, 128), F32),
                   jax.ShapeDtypeStruct((1, 128), F32)],
        scratch_shapes=[pltpu.VMEM((1, 128), F32)],
        compiler_params=_cparams(("arbitrary",)),
        name="out_route_" + mode,
    )(mab, mc, x, wo, gffn, wr, br, cnt_in)


DISPATCH_BATCH = 256
DISPATCH_UNROLL = 4


def _dispatch_kernel(pos_ref, h2p_ref, h2s_ref, xs_in_ref, xs_ref, sem):
    del xs_in_ref

    def copy_token(src_ref, t, slot):
        for k in range(2):
            pltpu.make_async_copy(src_ref.at[t], xs_ref.at[pos_ref[slot + k]], sem).start()

    def wait_rows(n):
        pltpu.make_async_copy(xs_ref.at[pl.ds(0, n)], xs_ref.at[pl.ds(0, n)], sem).wait()

    def batch(b, carry):
        def body(j, c):
            for u in range(DISPATCH_UNROLL):
                t = b * DISPATCH_BATCH + j * DISPATCH_UNROLL + u
                copy_token(h2p_ref, t, 2 * t)
            return c
        lax.fori_loop(0, DISPATCH_BATCH // DISPATCH_UNROLL, body, 0)

        @pl.when(b > 0)
        def _():
            wait_rows(2 * DISPATCH_BATCH)
        return carry

    lax.fori_loop(0, T // DISPATCH_BATCH, batch, 0)

    def sample_body(j, c):
        copy_token(h2s_ref, j, 2 * (T + j))
        return c
    lax.fori_loop(0, NS, sample_body, 0)
    wait_rows(2 * DISPATCH_BATCH)
    wait_rows(2 * NS)


def _dispatch(pos, h2p, h2s):
    xs0 = jnp.zeros((S_PAD, 4, 128), U32)
    grid_spec = pltpu.PrefetchScalarGridSpec(
        num_scalar_prefetch=1,
        grid=(1,),
        in_specs=[pl.BlockSpec(memory_space=pl.ANY)] * 3,
        out_specs=pl.BlockSpec(memory_space=pl.ANY),
        scratch_shapes=[pltpu.SemaphoreType.DMA(())],
    )
    return pl.pallas_call(
        _dispatch_kernel,
        grid_spec=grid_spec,
        out_shape=jax.ShapeDtypeStruct((S_PAD, 4, 128), U32),
        input_output_aliases={3: 0},
        compiler_params=_cparams(("arbitrary",)),
        name="dispatch",
    )(pos, h2p, h2s, xs0)


def _expert_kernel(te_ref, tc_ref, xs_ref, wg_ref, wu_ref, wd_ref, y_ref, wg_b, wu_b, wd_b):
    t = pl.program_id(0)
    prev = te_ref[jnp.maximum(t - 1, 0)]

    @pl.when((t == 0) | (te_ref[t] != prev))
    def _():
        wg_b[...] = wg_ref[...].astype(BF16)
        wu_b[...] = wu_ref[...].astype(BF16)
        wd_b[...] = wd_ref[...].astype(BF16)

    @pl.when(tc_ref[t] > 0)
    def _():
        packed = jnp.concatenate([xs_ref[:, c, :] for c in range(4)], axis=1)
        lo, hi = _unpack_bf16_pairs(packed)
        a = _dot(lo, wg_b[0:D // 2, :]) + _dot(hi, wg_b[D // 2:, :])
        b = _dot(lo, wu_b[0:D // 2, :]) + _dot(hi, wu_b[D // 2:, :])
        act = (a * _sigmoid(a) * b).astype(BF16)
        y = _dot(act, wd_b[...])
        for c in range(8):
            y_ref[:, c, :] = y[:, c * 128:(c + 1) * 128]

    @pl.when(tc_ref[t] == 0)
    def _():
        y_ref[...] = jnp.zeros_like(y_ref)


def _experts(tile_e, tile_cnt, xs_sorted, wg, wu, wd, layer):
    grid_spec = pltpu.PrefetchScalarGridSpec(
        num_scalar_prefetch=2,
        grid=(NT,),
        in_specs=[pl.BlockSpec((TMX, 4, 128), lambda t, te, tc: (t, 0, 0)),
                  pl.BlockSpec((None, D, DE), lambda t, te, tc: (layer * NE + te[t], 0, 0)),
                  pl.BlockSpec((None, D, DE), lambda t, te, tc: (layer * NE + te[t], 0, 0)),
                  pl.BlockSpec((None, DE, D), lambda t, te, tc: (layer * NE + te[t], 0, 0))],
        out_specs=pl.BlockSpec((TMX, 8, 128), lambda t, te, tc: (t, 0, 0)),
        scratch_shapes=[pltpu.VMEM((D, DE), BF16), pltpu.VMEM((D, DE), BF16), pltpu.VMEM((DE, D), BF16)],
    )
    return pl.pallas_call(
        _expert_kernel,
        grid_spec=grid_spec,
        out_shape=jax.ShapeDtypeStruct((S_PAD, 8, 128), F32),
        compiler_params=_cparams(("arbitrary",)),
        name="experts",
    )(tile_e, tile_cnt, xs_sorted, wg, wu, wd)


def _combine_ple_kernel(pos_ref, xmid_ref, info_ref, p_ref, wple_ref, wpg_ref, gple_ref, gfin_ref, ys_ref,
                        x_ref, y_ref, gbuf, x2s, sem, *, mode, tm, slot0):
    i = pl.program_id(0)
    n = pl.num_programs(0)

    def issue(tile, buf):
        def body(j, c):
            s = slot0 + 2 * (tile * tm + j)
            pltpu.make_async_copy(ys_ref.at[pos_ref[s]], gbuf.at[buf, j], sem.at[buf]).start()
            pltpu.make_async_copy(ys_ref.at[pos_ref[s + 1]], gbuf.at[buf, tm + j], sem.at[buf]).start()
            return c
        lax.fori_loop(0, tm, body, 0)

    @pl.when(i == 0)
    def _():
        issue(0, 0)

    @pl.when(i + 1 < n)
    def _():
        issue(i + 1, (i + 1) % 2)

    buf = i % 2
    pltpu.make_async_copy(gbuf.at[buf], gbuf.at[buf], sem.at[buf]).wait()

    info = info_ref[...]
    w1 = info[:, 2:3]
    w2 = info[:, 3:4]
    for c in range(8):
        cs = slice(c * 128, (c + 1) * 128)
        x2s[:, cs] = xmid_ref[:, cs] + w1 * gbuf[buf, 0:tm, c, :] + w2 * gbuf[buf, tm:2 * tm, c, :]
    x2 = x2s[...]
    gate = _sigmoid(_mm(_rms(x2, gple_ref[...]), wpg_ref[...], mode))
    x3 = x2 + _mm(p_ref[...], wple_ref[...], mode) * gate
    x_ref[...] = x3
    y_ref[...] = _rms(x3, gfin_ref[...])


def _combine_ple(pos, xmid, info, p, wple, wpg, gple, gfin, y_sorted, *, mode, tm, slot0):
    n_tok = xmid.shape[0]
    full = lambda shape: pl.BlockSpec(shape, lambda i, pos: (0,) * len(shape))
    tok = lambda w: pl.BlockSpec((tm, w), lambda i, pos: (i, 0))
    grid_spec = pltpu.PrefetchScalarGridSpec(
        num_scalar_prefetch=1,
        grid=(n_tok // tm,),
        in_specs=[tok(D), tok(128), tok(D_PLE), full((D_PLE, D)), full((D, D)), full((1, D)), full((1, D)),
                  pl.BlockSpec(memory_space=pl.ANY)],
        out_specs=[tok(D), tok(D)],
        scratch_shapes=[pltpu.VMEM((2, 2 * tm, 8, 128), F32), pltpu.VMEM((tm, D), F32),
                        pltpu.SemaphoreType.DMA((2,))],
    )
    return pl.pallas_call(
        functools.partial(_combine_ple_kernel, mode=mode, tm=tm, slot0=slot0),
        grid_spec=grid_spec,
        out_shape=[jax.ShapeDtypeStruct((n_tok, D), F32), jax.ShapeDtypeStruct((n_tok, D), F32)],
        compiler_params=_cparams(("arbitrary",)),
        name="combine_ple_" + mode,
    )(pos, xmid, info, p, wple, wpg, gple, gfin, y_sorted)


def _sample_in_kernel(x_ref, gmix_ref, win_ref, convw_ref, s0_ref, s1_ref, lng_ref, lnb_ref,
                      ws0_ref, bs0_ref, bf_ref, gout_ref,
                      mab_ref, q_ref, k_ref, v_ref, logf_ref, z_ref, vn_ref):
    h = _rms(x_ref[...], gmix_ref[...])
    h_hi, h_lo = _split(h)

    def proj(lo, hi):
        w_hi, w_lo = _split(win_ref[:, lo:hi])
        return _dot(h_hi, w_hi) + _dot(h_lo, w_hi) + _dot(h_hi, w_lo)

    pa = proj(0, 3 * W_A)
    z = pa[:, W_A:2 * W_A] * pa[:, 2 * W_A:3 * W_A]
    cw = convw_ref[...]
    y_a = pa[:, 0:W_A] * (cw[0:1, :] * s0_ref[...] + cw[1:2, :] * s1_ref[...] + cw[2:3, :] * z)
    z_ref[...] = z
    gout = gout_ref[...]
    mab_ref[:, 0:W_A] = _rms(y_a, gout[:, 0:W_A])

    o = 3 * W_A
    pb = proj(o, o + 2 * W_B)
    v_b = pb[:, W_B:2 * W_B]
    mu = jnp.mean(v_b, axis=-1, keepdims=True)
    vc = v_b - mu
    var = jnp.mean(vc * vc, axis=-1, keepdims=True)
    vn = vc * lax.rsqrt(var + EPS) * lng_ref[...] + lnb_ref[...]
    vn_ref[...] = vn
    y_b = pb[:, 0:W_B] * (ws0_ref[...] * vn + bs0_ref[...])
    mab_ref[:, W_A:W_A + W_B] = _rms(y_b, gout[:, W_A:W_A + W_B])

    o = 3 * W_A + 2 * W_B
    q_ref[...] = proj(o, o + W_C) * (HD ** -0.5)
    k_ref[...] = proj(o + W_C, o + 2 * W_C)
    v_ref[...] = proj(o + 2 * W_C, o + 3 * W_C)
    f = proj(o + 3 * W_C, D_IN_PAD)
    logf_ref[...] = _log_sigmoid(f + bf_ref[...])


def _sample_in(x, gmix, win_pad, convw, s0, s1, lng, lnb, ws0, bs0, bf_pad, gout):
    shapes = [(NS, W_A + W_B), (NS, W_C), (NS, W_C), (NS, W_C), (NS, 128), (NS, W_A), (NS, W_B)]
    return pl.pallas_call(
        _sample_in_kernel,
        out_shape=[jax.ShapeDtypeStruct(s, F32) for s in shapes],
        compiler_params=pltpu.CompilerParams(vmem_limit_bytes=VMEM_LIMIT),
        name="sample_in",
    )(x, gmix, win_pad, convw, s0, s1, lng, lnb, ws0, bs0, bf_pad, gout)


def _sample_attn_kernel(pt_ref, q_ref, kn_ref, vn_ref, lfn_ref, gc_ref, *rest):
    k_refs = rest[0:PP]
    v_refs = rest[PP:2 * PP]
    lf_refs = rest[2 * PP:3 * PP]
    o_ref = rest[3 * PP]
    m_s, l_s, acc_s, tot_s = rest[3 * PP + 1:]
    j = pl.program_id(1)

    @pl.when(j == 0)
    def _():
        m_s[...] = jnp.full_like(m_s, NEG)
        l_s[...] = jnp.zeros_like(l_s)
        acc_s[...] = jnp.zeros_like(acc_s)
        tot_s[...] = jnp.zeros_like(tot_s)

    q = q_ref[...]
    cn = lfn_ref[...]
    lane = lax.broadcasted_iota(I32, (H_C, PAGE), 1)
    half = PAGE // 2
    on_diag = (lax.broadcasted_iota(I32, (half, H_C, HD), 0) == lax.broadcasted_iota(I32, (half, H_C, HD), 2))

    m = m_s[...]
    l = l_s[...]
    acc = acc_s[...]
    tot = tot_s[...]
    for pp in reversed(range(PP)):
        lf = lf_refs[pp][...]
        inc = lf
        sh = 1
        while sh < PAGE:
            inc = inc + jnp.where(lane >= sh, pltpu.roll(inc, sh, axis=1), 0.0)
            sh *= 2
        page_tot = inc[:, PAGE - 1:PAGE]
        bias = (page_tot - inc) + tot + cn
        tot = tot + page_tot
        bias_hi = pltpu.roll(bias, half, axis=1)
        k_ref = k_refs[pp]
        v_ref = v_refs[pp]
        s_lo = jnp.sum(k_ref[0:half] * q[None] + jnp.where(on_diag, bias[None, :, 0:half], 0.0),
                       axis=-1, keepdims=True)
        s_hi = jnp.sum(k_ref[half:PAGE] * q[None] + jnp.where(on_diag, bias_hi[None, :, 0:half], 0.0),
                       axis=-1, keepdims=True)
        mn = jnp.maximum(m, jnp.maximum(jnp.max(s_lo, axis=0), jnp.max(s_hi, axis=0)))
        a = jnp.exp(m - mn)
        p_lo = jnp.exp(s_lo - mn[None])
        p_hi = jnp.exp(s_hi - mn[None])
        l = a * l + jnp.sum(p_lo, axis=0) + jnp.sum(p_hi, axis=0)
        acc = a * acc + jnp.sum(p_lo * v_ref[0:half], axis=0) + jnp.sum(p_hi * v_ref[half:PAGE], axis=0)
        m = mn
    m_s[...] = m
    l_s[...] = l
    acc_s[...] = acc
    tot_s[...] = tot

    @pl.when(j == NPG - 1)
    def _():
        s_new = jnp.sum(q * kn_ref[...], axis=-1, keepdims=True)
        mn = jnp.maximum(m, s_new)
        a = jnp.exp(m - mn)
        pn = jnp.exp(s_new - mn)
        o8 = (a * acc + pn * vn_ref[...]) / (a * l + pn)
        ms = jnp.sum(jnp.sum(o8 * o8, axis=-1, keepdims=True), axis=0, keepdims=True) / W_C
        o_ref[...] = o8 * lax.rsqrt(ms + EPS) * gc_ref[...]


def _sample_attn(pt_flat, q3, kn3, vn3, lfn3, gc, ck, cv, clft, layer):
    def page_map(nd):
        return lambda pp: (lambda n, j, pt: (layer, pt[n * N_PAGES + (NPG - 1 - j) * PP + pp]) + (0,) * nd)

    head = pl.BlockSpec((None, H_C, HD), lambda n, j, pt: (n, 0, 0))
    in_specs = [head, head, head,
                pl.BlockSpec((None, H_C, 1), lambda n, j, pt: (n, 0, 0)),
                pl.BlockSpec((H_C, HD), lambda n, j, pt: (0, 0))]
    in_specs += [pl.BlockSpec((None, None, PAGE, H_C, HD), page_map(3)(pp)) for pp in range(PP)]
    in_specs += [pl.BlockSpec((None, None, PAGE, H_C, HD), page_map(3)(pp)) for pp in range(PP)]
    in_specs += [pl.BlockSpec((None, None, H_C, PAGE), page_map(2)(pp)) for pp in range(PP)]
    grid_spec = pltpu.PrefetchScalarGridSpec(
        num_scalar_prefetch=1,
        grid=(NS, NPG),
        in_specs=in_specs,
        out_specs=head,
        scratch_shapes=[pltpu.VMEM((H_C, 1), F32), pltpu.VMEM((H_C, 1), F32),
                        pltpu.VMEM((H_C, HD), F32), pltpu.VMEM((H_C, 1), F32)],
    )
    return pl.pallas_call(
        _sample_attn_kernel,
        grid_spec=grid_spec,
        out_shape=jax.ShapeDtypeStruct((NS, H_C, HD), F32),
        compiler_params=_cparams(("arbitrary", "arbitrary")),
        name="sample_attn",
    )(pt_flat, q3, kn3, vn3, lfn3, gc, *([ck] * PP), *([cv] * PP), *([clft] * PP))


def _schedule(cnt_f, info_all):
    cnt = cnt_f[0, :NE].astype(I32)
    ntile = (cnt + TMX - 1) // TMX
    tcum = jnp.cumsum(ntile).astype(I32)
    tbase = tcum - ntile
    t = jnp.arange(NT, dtype=I32)
    tile_e = jnp.minimum(jnp.sum((tcum[None, :] <= t[:, None]).astype(I32), axis=1), NE - 1)
    sel = tile_e[:, None] == jnp.arange(NE, dtype=I32)[None, :]
    within = t - jnp.sum(jnp.where(sel, tbase[None, :], 0), axis=1)
    tile_cnt = jnp.clip(jnp.sum(jnp.where(sel, cnt[None, :], 0), axis=1) - within * TMX, 0, TMX).astype(I32)
    eid = info_all[:, 0:2].astype(I32)
    rank = info_all[:, 4:6].astype(I32)
    base = jnp.sum(jnp.where(eid[:, :, None] == jnp.arange(NE, dtype=I32)[None, None, :],
                             (tbase * TMX)[None, None, :], 0), axis=-1)
    pos = (base + rank).reshape(-1).astype(I32)
    return tile_e, tile_cnt, pos


def kernel(x_prompt, x_sample, p_prompt, p_sample, cache_k, cache_v, cache_logf, state_conv, page_table, g_mix, w_in, conv_w, ln_g, ln_b, w_s, b_s, b_f, g_out, w_o, g_ffn, w_grp, b_grp, w_rt, b_rt, w_gate, w_up, w_down, g_ple, w_ple, w_ple_gate, g_final):
    n_pool = cache_k.shape[1]
    xp = x_prompt.reshape(T, D)
    xs = x_sample.reshape(NS, D)
    pt_flat = page_table.reshape(-1).astype(I32)
    del n_pool
    clft = jnp.swapaxes(cache_logf, 2, 3)
    zero_cnt = jnp.zeros((1, 128), F32)
    wg_all = w_gate.reshape(DEPTH * NE, D, DE)
    wu_all = w_up.reshape(DEPTH * NE, D, DE)
    wd_all = w_down.reshape(DEPTH * NE, DE, D)
    gfin = g_final.reshape(1, D)

    outs = {k: [] for k in ("kp", "vp", "lfp", "cvp", "ks", "vs", "lfs", "cvs", "chv")}
    yp = ys = None
    for i in range(DEPTH):
        gmix = g_mix[i].reshape(1, D)
        gout = g_out[i].reshape(1, D)
        gffn = g_ffn[i].reshape(1, D)
        gple = g_ple[i].reshape(1, D)
        win_pad = jnp.pad(w_in[i], ((0, 0), (0, D_IN_PAD - D_IN)))
        win_bf = win_pad.astype(BF16)
        bf_pad = jnp.pad(b_f[i], (0, 128 - H_C)).reshape(1, 128)
        lng = ln_g[i].reshape(1, W_B)
        lnb = ln_b[i].reshape(1, W_B)
        bse = jnp.repeat(b_s[i].T, HD, axis=1)
        ws0 = jnp.repeat(w_s[i][:, 0, 0], HD).reshape(1, W_B)
        bs0 = jnp.repeat(b_s[i][:, 0], HD).reshape(1, W_B)
        wr = jnp.pad(jnp.concatenate([w_grp[i], w_rt[i].reshape(D, NE)], axis=1), ((0, 0), (0, 128 - NG - NE)))
        br = jnp.pad(jnp.concatenate([b_grp[i], b_rt[i].reshape(NE)]), (0, 128 - NG - NE)).reshape(1, 128)

        mab, qbf, kbf, vbf, k_p, v_p, lf_p, c_p, ctail = _prompt_in(
            xp, gmix, win_bf, conv_w[i], lng, lnb, w_s[i], bse, bf_pad, gout)
        ct = jnp.swapaxes(c_p.reshape(NB, L, H_C), 1, 2)
        mc = _prompt_attn(qbf, kbf, vbf, c_p, ct, gout)
        xmid_p, h2_p, info_p, cnt_p = _out_route(mab, mc, xp, w_o[i].astype(BF16), gffn, wr.astype(BF16), br,
                                                 zero_cnt, mode="bf16", tm=TM_C)

        mab_s, q_s, k_s, v_s, lf_s128, z_s, vn_s = _sample_in(
            xs, gmix, win_pad, conv_w[i], state_conv[i, :, 0], state_conv[i, :, 1], lng, lnb, ws0, bs0,
            bf_pad, gout)
        lf_s = lf_s128[:, :H_C]
        mc_s = _sample_attn(pt_flat, q_s.reshape(NS, H_C, HD), k_s.reshape(NS, H_C, HD),
                            v_s.reshape(NS, H_C, HD), lf_s.reshape(NS, H_C, 1),
                            gout[:, W_A + W_B:].reshape(H_C, HD), cache_k, cache_v, clft, i)
        xmid_s, h2_s, info_s, cnt_all = _out_route(mab_s, mc_s.reshape(NS, W_C), xs, w_o[i], gffn, wr, br,
                                                   cnt_p, mode="x3", tm=NS)

        tile_e, tile_cnt, pos = _schedule(cnt_all, jnp.concatenate([info_p[:, :8], info_s[:, :8]], axis=0))
        xs_sorted = _dispatch(pos, h2_p, h2_s)
        y_sorted = _experts(tile_e, tile_cnt, xs_sorted, wg_all, wu_all, wd_all, i)

        xp, yp = _combine_ple(pos, xmid_p, info_p, p_prompt[i].reshape(T, D_PLE), w_ple[i].astype(BF16),
                              w_ple_gate[i].astype(BF16), gple, gfin, y_sorted, mode="bf16", tm=TM_F, slot0=0)
        xs, ys = _combine_ple(pos, xmid_s, info_s, p_sample[i].reshape(NS, D_PLE), w_ple[i],
                              w_ple_gate[i], gple, gfin, y_sorted, mode="x3", tm=NS, slot0=2 * T)

        outs["kp"].append(k_p.reshape(NB, L, H_C, HD))
        outs["vp"].append(v_p.reshape(NB, L, H_C, HD))
        outs["lfp"].append(lf_p.reshape(NB, L, H_C))
        outs["cvp"].append(ctail)
        outs["ks"].append(k_s.reshape(NS, 1, H_C, HD))
        outs["vs"].append(v_s.reshape(NS, 1, H_C, HD))
        outs["lfs"].append(lf_s.reshape(NS, 1, H_C))
        outs["cvs"].append(jnp.stack([state_conv[i, :, 1], z_s], axis=1))
        outs["chv"].append(vn_s.reshape(NS, 1, W_B))

    st = lambda k: jnp.stack(outs[k])
    return (yp.reshape(NB, L, D), ys.reshape(NS, 1, D), st("kp"), st("vp"), st("lfp"), st("cvp"),
            st("ks"), st("vs"), st("lfs"), st("cvs"), st("chv"))
```

```python
import functools

import jax
import jax.numpy as jnp
from jax import lax
from jax.experimental import pallas as pl
from jax.experimental.pallas import tpu as pltpu

F32 = jnp.float32
BF16 = jnp.bfloat16
I32 = jnp.int32

D = 1024
NB = 8
L = 2048
T = NB * L
DEPTH = 2
NS = 32
PAGE = 128
N_PAGES = 64
H_C = 8
HD = 64
W_A = 256
W_B = 256
W_C = 512
D_IN = 2824
D_IN_PAD = 2944
NG = 4
EPG = 8
NE = NG * EPG
DE = 512
D_PLE = 256
CHUNK = 128
EPS = 1e-6
NEG = -jnp.inf

T_ALL = T + NS
S_SLOTS = 2 * T_ALL
TMX = 256
NT = S_SLOTS // TMX + NE
S_PAD = NT * TMX

TM_A = 512
TQ = 512
TM_C = 512
TM_F = 512
PP = 16
NPG = N_PAGES // PP

VMEM_LIMIT = 56 * 1024 * 1024


def _cparams(sem):
    return pltpu.CompilerParams(dimension_semantics=sem, vmem_limit_bytes=VMEM_LIMIT)


def _split(a):
    hi = a.astype(BF16)
    lo = (a - hi.astype(F32)).astype(BF16)
    return hi, lo


def _dot(a, b):
    return jnp.dot(a, b, preferred_element_type=F32)


def _dot_nt(a, b):
    return lax.dot_general(a, b, (((1,), (1,)), ((), ())), preferred_element_type=F32)


def _mm(a, b, mode):
    if mode == "bf16":
        return _dot(a.astype(BF16), b.astype(BF16))
    a_hi, a_lo = _split(a)
    b_hi, b_lo = _split(b)
    return _dot(a_hi, b_hi) + _dot(a_lo, b_hi) + _dot(a_hi, b_lo)


def _rms(x, g):
    return x * lax.rsqrt(jnp.mean(x * x, axis=-1, keepdims=True) + EPS) * g


def _log_sigmoid(x):
    return jnp.minimum(x, 0.0) - jnp.log1p(jnp.exp(-jnp.abs(x)))


def _sigmoid(x):
    return 1.0 / (1.0 + jnp.exp(-x))


def _prompt_in_kernel(x_ref, gmix_ref, win_ref, convw_ref, lng_ref, lnb_ref, ws_ref, bse_ref,
                      bf_ref, gout_ref,
                      mab_ref, qbf_ref, kbf_ref, vbf_ref, k_ref, v_ref, logf_ref, c_ref, ctail_ref,
                      zbuf, ccar):
    tm = TM_A
    i = pl.program_id(0)

    @pl.when(i % (L // tm) == 0)
    def _():
        zbuf[0:8, :] = jnp.zeros((8, W_A), F32)
        ccar[...] = jnp.zeros_like(ccar)

    hb = _rms(x_ref[...], gmix_ref[...]).astype(BF16)

    def proj(lo, hi):
        return _dot(hb, win_ref[:, lo:hi])

    pa = proj(0, 3 * W_A)
    z = pa[:, W_A:2 * W_A] * pa[:, 2 * W_A:3 * W_A]
    zbuf[8:8 + tm, :] = z
    z1 = zbuf[7:7 + tm, :]
    z2 = zbuf[6:6 + tm, :]
    cw = convw_ref[...]
    y_a = pa[:, 0:W_A] * (cw[0:1, :] * z2 + cw[1:2, :] * z1 + cw[2:3, :] * z)
    tail = z[tm - 2:tm, :]
    ctail_ref[...] = tail
    zbuf[6:8, :] = tail
    gout = gout_ref[...]
    mab_ref[:, 0:W_A] = _rms(y_a, gout[:, 0:W_A]).astype(BF16)

    o = 3 * W_A
    pb = proj(o, o + 2 * W_B)
    u_b = pb[:, 0:W_B]
    v_b = pb[:, W_B:2 * W_B]
    mu = jnp.mean(v_b, axis=-1, keepdims=True)
    vc = v_b - mu
    var = jnp.mean(vc * vc, axis=-1, keepdims=True)
    vn = vc * lax.rsqrt(var + EPS) * lng_ref[...] + lnb_ref[...]
    vnb = vn.astype(BF16)
    r_i = lax.broadcasted_iota(I32, (CHUNK, CHUNK), 0)
    c_i = lax.broadcasted_iota(I32, (CHUNK, CHUNK), 1)
    lane_head = lax.broadcasted_iota(I32, (CHUNK, W_B), 1) // HD
    ws_t = [jnp.where(r_i >= c_i, ws_ref[hh], 0.0).astype(BF16) for hh in range(4)]
    parts = []
    for cidx in range(tm // CHUNK):
        vchunk = vnb[cidx * CHUNK:(cidx + 1) * CHUNK, :]
        sc = jnp.zeros((CHUNK, W_B), F32)
        for hh in range(4):
            sc = jnp.where(lane_head == hh, _dot(ws_t[hh], vchunk), sc)
        parts.append(sc + bse_ref[...])
    s = jnp.concatenate(parts, axis=0)
    y_b = u_b * s
    mab_ref[:, W_A:W_A + W_B] = _rms(y_b, gout[:, W_A:W_A + W_B]).astype(BF16)

    o = 3 * W_A + 2 * W_B
    q = proj(o, o + W_C)
    k = proj(o + W_C, o + 2 * W_C)
    v = proj(o + 2 * W_C, o + 3 * W_C)
    f = proj(o + 3 * W_C, D_IN_PAD)
    qbf_ref[...] = (q * (HD ** -0.5)).astype(BF16)
    k_ref[...] = k
    v_ref[...] = v
    kbf_ref[...] = k.astype(BF16)
    vbf_ref[...] = v.astype(BF16)
    lf = _log_sigmoid(f + bf_ref[...])
    logf_ref[...] = lf[:, 0:H_C]

    p1 = lf.astype(BF16)
    r1 = lf - p1.astype(F32)
    p2 = r1.astype(BF16)
    p3 = (r1 - p2.astype(F32)).astype(BF16)
    tr = lax.broadcasted_iota(I32, (tm, tm), 0) >= lax.broadcasted_iota(I32, (tm, tm), 1)
    trb = jnp.where(tr, 1.0, 0.0).astype(BF16)
    cs = _dot(trb, p1) + _dot(trb, p2) + _dot(trb, p3) + ccar[...]
    c_ref[...] = cs[:, 0:H_C]
    ccar[...] = cs[tm - 1:tm, :]


def _prompt_in(x, gmix, win_bf, convw, lng, lnb, ws, bse, bf_pad, gout):
    tm = TM_A
    n = T // tm
    full = lambda shape: pl.BlockSpec(shape, lambda i: (0,) * len(shape))
    tok = lambda w: pl.BlockSpec((tm, w), lambda i: (i, 0))
    return pl.pallas_call(
        _prompt_in_kernel,
        grid=(n,),
        in_specs=[tok(D), full((1, D)), full((D, D_IN_PAD)), full((3, W_A)), full((1, W_B)),
                  full((1, W_B)), full((4, CHUNK, CHUNK)), full((CHUNK, W_B)), full((1, 128)),
                  full((1, D))],
        out_specs=[tok(W_A + W_B), tok(W_C), tok(W_C), tok(W_C), tok(W_C), tok(W_C), tok(H_C), tok(H_C),
                   pl.BlockSpec((None, 2, W_A), lambda i: (i // (L // tm), 0, 0))],
        out_shape=[jax.ShapeDtypeStruct((T, W_A + W_B), BF16),
                   jax.ShapeDtypeStruct((T, W_C), BF16),
                   jax.ShapeDtypeStruct((T, W_C), BF16),
                   jax.ShapeDtypeStruct((T, W_C), BF16),
                   jax.ShapeDtypeStruct((T, W_C), F32),
                   jax.ShapeDtypeStruct((T, W_C), F32),
                   jax.ShapeDtypeStruct((T, H_C), F32),
                   jax.ShapeDtypeStruct((T, H_C), F32),
                   jax.ShapeDtypeStruct((NB, 2, W_A), F32)],
        scratch_shapes=[pltpu.VMEM((tm + 8, W_A), F32), pltpu.VMEM((1, 128), F32)],
        compiler_params=_cparams(("arbitrary",)),
        name="prompt_in",
    )(x, gmix, win_bf, convw, lng, lnb, ws, bse, bf_pad, gout)


def _prompt_attn_kernel(q_ref, k_ref, v_ref, c_ref, ct_ref, gout_ref, o_ref, yc):
    tq = TQ
    qi = pl.program_id(1)
    left = lax.broadcasted_iota(I32, (1, 128), 1) < HD
    row = lax.broadcasted_iota(I32, (tq, tq), 0)
    col = lax.broadcasted_iota(I32, (tq, tq), 1)
    zero = jnp.zeros((), BF16)

    for hp in range(H_C // 2):
        ls = slice(hp * 128, (hp + 1) * 128)
        q2 = q_ref[:, ls]
        q_a = jnp.where(left, q2, zero)
        q_b = jnp.where(left, zero, q2)
        h_a, h_b = 2 * hp, 2 * hp + 1
        cq_a = c_ref[:, h_a:h_a + 1]
        cq_b = c_ref[:, h_b:h_b + 1]

        def block(ki, carry, masked):
            m_a, l_a, m_b, l_b, acc = carry
            ks = pl.multiple_of(ki * tq, tq)
            k2 = k_ref[pl.ds(ks, tq), ls]
            v2 = v_ref[pl.ds(ks, tq), ls]

            def one(qh, cq, h, m, l):
                s = _dot_nt(qh, k2) + cq - ct_ref[h:h + 1, pl.ds(ks, tq)]
                if masked:
                    s = jnp.where(col <= row, s, NEG)
                mn = jnp.maximum(m, jnp.max(s, axis=-1, keepdims=True))
                a = jnp.exp(m - mn)
                p = jnp.exp(s - mn)
                ln = a * l + jnp.sum(p, axis=-1, keepdims=True)
                return mn, ln, a, _dot(p.astype(BF16), v2)

            m_a, l_a, a_a, pv_a = one(q_a, cq_a, h_a, m_a, l_a)
            m_b, l_b, a_b, pv_b = one(q_b, cq_b, h_b, m_b, l_b)
            acc = acc * jnp.where(left, a_a, a_b) + jnp.where(left, pv_a, pv_b)
            return m_a, l_a, m_b, l_b, acc

        init = (jnp.full((tq, 1), NEG, F32), jnp.zeros((tq, 1), F32),
                jnp.full((tq, 1), NEG, F32), jnp.zeros((tq, 1), F32),
                jnp.zeros((tq, 128), F32))
        carry = lax.fori_loop(0, qi, lambda ki, c: block(ki, c, False), init)
        m_a, l_a, m_b, l_b, acc = block(qi, carry, True)
        yc[:, ls] = acc / jnp.where(left, l_a, l_b)

    o_ref[...] = _rms(yc[...], gout_ref[:, W_A + W_B:]).astype(BF16)


def _prompt_attn(qbf, kbf, vbf, c, ct, gout):
    nq = L // TQ
    return pl.pallas_call(
        _prompt_attn_kernel,
        grid=(NB, nq),
        in_specs=[pl.BlockSpec((TQ, W_C), lambda n, qi: (n * nq + qi, 0)),
                  pl.BlockSpec((L, W_C), lambda n, qi: (n, 0)),
                  pl.BlockSpec((L, W_C), lambda n, qi: (n, 0)),
                  pl.BlockSpec((TQ, H_C), lambda n, qi: (n * nq + qi, 0)),
                  pl.BlockSpec((None, H_C, L), lambda n, qi: (n, 0, 0)),
                  pl.BlockSpec((1, D), lambda n, qi: (0, 0))],
        out_specs=pl.BlockSpec((TQ, W_C), lambda n, qi: (n * nq + qi, 0)),
        out_shape=jax.ShapeDtypeStruct((T, W_C), BF16),
        scratch_shapes=[pltpu.VMEM((TQ, W_C), F32)],
        compiler_params=_cparams(("arbitrary", "arbitrary")),
        name="prompt_attn",
    )(qbf, kbf, vbf, c, ct, gout)


def _route(logits):
    lane = lax.broadcasted_iota(I32, logits.shape, 1).astype(F32)
    big = jnp.float32(1e9)
    gl = jnp.where(lane < NG, logits, NEG)
    gmax = jnp.max(gl, axis=-1, keepdims=True)
    p_g = 1.0 / jnp.sum(jnp.exp(gl - gmax), axis=-1, keepdims=True)
    g_idx = jnp.min(jnp.where(gl == gmax, lane, big), axis=-1, keepdims=True)
    lo = NG + EPG * g_idx
    el = jnp.where((lane >= lo) & (lane < lo + EPG), logits, NEG)
    m1 = jnp.max(el, axis=-1, keepdims=True)
    i1 = jnp.min(jnp.where(el == m1, lane, big), axis=-1, keepdims=True)
    zsum = jnp.sum(jnp.exp(el - m1), axis=-1, keepdims=True)
    el2 = jnp.where(lane == i1, NEG, el)
    m2 = jnp.max(el2, axis=-1, keepdims=True)
    i2 = jnp.min(jnp.where(el2 == m2, lane, big), axis=-1, keepdims=True)
    p1 = 1.0 / zsum
    p2 = jnp.exp(m2 - m1) / zsum
    den = p1 + p2
    w1 = p_g * p1 / den
    w2 = p_g * p2 / den
    e1 = i1 - NG
    e2 = i2 - NG
    info = jnp.where(lane == 0, e1,
                     jnp.where(lane == 1, e2,
                               jnp.where(lane == 2, w1, jnp.where(lane == 3, w2, 0.0))))
    return info, e1, e2


def _out_route_kernel(mab_ref, mc_ref, x_ref, wo_ref, gffn_ref, wr_ref, br_ref, cin_ref,
                      xmid_ref, h2_ref, info_ref, cnt_ref, carry, *, mode, tm):
    @pl.when(pl.program_id(0) == 0)
    def _():
        carry[...] = cin_ref[...]

    upd = _mm(mab_ref[...], wo_ref[0:W_A + W_B, :], mode) + _mm(mc_ref[...], wo_ref[W_A + W_B:, :], mode)
    xm = x_ref[...] + upd
    xmid_ref[...] = xm
    h2 = _rms(xm, gffn_ref[...])
    for c in range(D // 128):
        h2_ref[:, c, :] = h2[:, c * 128:(c + 1) * 128]
    logits = _mm(h2, wr_ref[...], mode) + br_ref[...]
    info, e1, e2 = _route(logits)

    lane = lax.broadcasted_iota(I32, (tm, 128), 1).astype(F32)
    hit1 = lane == e1
    hit2 = lane == e2
    both = jnp.where(hit1, 1.0, 0.0) + jnp.where(hit2, 1.0, 0.0)
    earlier = lax.broadcasted_iota(I32, (tm, tm), 0) > lax.broadcasted_iota(I32, (tm, tm), 1)
    before = _dot(jnp.where(earlier, 1.0, 0.0).astype(BF16), both.astype(BF16)) + carry[...]
    r1 = jnp.sum(jnp.where(hit1, before, 0.0), axis=-1, keepdims=True)
    r2 = jnp.sum(jnp.where(hit2, before, 0.0), axis=-1, keepdims=True)
    info_ref[...] = jnp.where(lane == 4, r1, jnp.where(lane == 5, r2, info))
    total = carry[...] + jnp.sum(both, axis=0, keepdims=True)
    carry[...] = total
    cnt_ref[...] = total


def _out_route(mab, mc, x, wo, gffn, wr, br, cnt_in, *, mode, tm):
    n_tok = x.shape[0]
    full = lambda shape: pl.BlockSpec(shape, lambda i: (0,) * len(shape))
    tok = lambda w: pl.BlockSpec((tm, w), lambda i: (i, 0))
    return pl.pallas_call(
        functools.partial(_out_route_kernel, mode=mode, tm=tm),
        grid=(n_tok // tm,),
        in_specs=[tok(W_A + W_B), tok(W_C), tok(D), full((D, D)), full((1, D)), full((D, 128)),
                  full((1, 128)), full((1, 128))],
        out_specs=[tok(D), pl.BlockSpec((tm, D // 128, 128), lambda i: (i, 0, 0)), tok(128), full((1, 128))],
        out_shape=[jax.ShapeDtypeStruct((n_tok, D), F32),
                   jax.ShapeDtypeStruct((n_tok, D // 128, 128), F32),
                   jax.ShapeDtypeStruct((n_tok, 128), F32),
                   jax.ShapeDtypeStruct((1, 128), F32)],
        scratch_shapes=[pltpu.VMEM((1, 128), F32)],
        compiler_params=_cparams(("arbitrary",)),
        name="out_route_" + mode,
    )(mab, mc, x, wo, gffn, wr, br, cnt_in)


TM_D = 512
DISPATCH_UNROLL = 4


def _dispatch_kernel(pos_ref, h2p_ref, h2s_ref, xs_in_ref, xs_ref, sem):
    del xs_in_ref
    i = pl.program_id(0)
    last = i == pl.num_programs(0) - 1

    def copy_token(src_ref, r, slot):
        for k in range(2):
            pltpu.make_async_copy(src_ref.at[r], xs_ref.at[pos_ref[slot + k]], sem).start()

    def wait_rows(n):
        pltpu.make_async_copy(xs_ref.at[pl.ds(0, n)], xs_ref.at[pl.ds(0, n)], sem).wait()

    def body(j, c):
        for u in range(DISPATCH_UNROLL):
            r = j * DISPATCH_UNROLL + u
            copy_token(h2p_ref, r, 2 * (i * TM_D + r))
        return c
    lax.fori_loop(0, TM_D // DISPATCH_UNROLL, body, 0)

    @pl.when(last)
    def _():
        def sample_body(j, c):
            copy_token(h2s_ref, j, 2 * (T + j))
            return c
        lax.fori_loop(0, NS, sample_body, 0)
        wait_rows(2 * NS)

    wait_rows(2 * TM_D)


def _dispatch(pos, h2p, h2s):
    xs0 = jnp.zeros((S_PAD, D // 128, 128), F32)
    grid_spec = pltpu.PrefetchScalarGridSpec(
        num_scalar_prefetch=1,
        grid=(T // TM_D,),
        in_specs=[pl.BlockSpec((TM_D, D // 128, 128), lambda i, pos: (i, 0, 0)),
                  pl.BlockSpec((NS, D // 128, 128), lambda i, pos: (0, 0, 0)),
                  pl.BlockSpec(memory_space=pl.ANY)],
        out_specs=pl.BlockSpec(memory_space=pl.ANY),
        scratch_shapes=[pltpu.SemaphoreType.DMA(())],
    )
    return pl.pallas_call(
        _dispatch_kernel,
        grid_spec=grid_spec,
        out_shape=jax.ShapeDtypeStruct((S_PAD, D // 128, 128), F32),
        input_output_aliases={3: 0},
        compiler_params=_cparams(("arbitrary",)),
        name="dispatch",
    )(pos, h2p, h2s, xs0)


def _expert_kernel(te_ref, tc_ref, xs_ref, wg_ref, wu_ref, wd_ref, y_ref, wg_b, wu_b, wd_b):
    t = pl.program_id(0)
    prev = te_ref[jnp.maximum(t - 1, 0)]

    @pl.when((t == 0) | (te_ref[t] != prev))
    def _():
        wg_b[...] = wg_ref[...].astype(BF16)
        wu_b[...] = wu_ref[...].astype(BF16)
        wd_b[...] = wd_ref[...].astype(BF16)

    @pl.when(tc_ref[t] > 0)
    def _():
        xb = jnp.concatenate([xs_ref[:, c, :] for c in range(D // 128)], axis=1).astype(BF16)
        a = _dot(xb, wg_b[...])
        b = _dot(xb, wu_b[...])
        act = (a * _sigmoid(a) * b).astype(BF16)
        y = _dot(act, wd_b[...])
        for c in range(8):
            y_ref[:, c, :] = y[:, c * 128:(c + 1) * 128]

    @pl.when(tc_ref[t] == 0)
    def _():
        y_ref[...] = jnp.zeros_like(y_ref)


def _experts(tile_e, tile_cnt, xs_sorted, wg, wu, wd, layer):
    grid_spec = pltpu.PrefetchScalarGridSpec(
        num_scalar_prefetch=2,
        grid=(NT,),
        in_specs=[pl.BlockSpec((TMX, D // 128, 128), lambda t, te, tc: (t, 0, 0)),
                  pl.BlockSpec((None, D, DE), lambda t, te, tc: (layer * NE + te[t], 0, 0)),
                  pl.BlockSpec((None, D, DE), lambda t, te, tc: (layer * NE + te[t], 0, 0)),
                  pl.BlockSpec((None, DE, D), lambda t, te, tc: (layer * NE + te[t], 0, 0))],
        out_specs=pl.BlockSpec((TMX, 8, 128), lambda t, te, tc: (t, 0, 0)),
        scratch_shapes=[pltpu.VMEM((D, DE), BF16), pltpu.VMEM((D, DE), BF16), pltpu.VMEM((DE, D), BF16)],
    )
    return pl.pallas_call(
        _expert_kernel,
        grid_spec=grid_spec,
        out_shape=jax.ShapeDtypeStruct((S_PAD, 8, 128), F32),
        compiler_params=_cparams(("arbitrary",)),
        name="experts",
    )(tile_e, tile_cnt, xs_sorted, wg, wu, wd)


def _combine_ple_kernel(pos_ref, xmid_ref, info_ref, p_ref, wple_ref, wpg_ref, gple_ref, gfin_ref, ys_ref,
                        x_ref, y_ref, gbuf, x2s, sem, *, mode, tm, slot0):
    i = pl.program_id(0)
    n = pl.num_programs(0)

    def issue(tile, buf):
        def body(j, c):
            s = slot0 + 2 * (tile * tm + j)
            pltpu.make_async_copy(ys_ref.at[pos_ref[s]], gbuf.at[buf, j], sem.at[buf]).start()
            pltpu.make_async_copy(ys_ref.at[pos_ref[s + 1]], gbuf.at[buf, tm + j], sem.at[buf]).start()
            return c
        lax.fori_loop(0, tm, body, 0)

    @pl.when(i == 0)
    def _():
        issue(0, 0)

    @pl.when(i + 1 < n)
    def _():
        issue(i + 1, (i + 1) % 2)

    buf = i % 2
    pltpu.make_async_copy(gbuf.at[buf], gbuf.at[buf], sem.at[buf]).wait()

    info = info_ref[...]
    w1 = info[:, 2:3]
    w2 = info[:, 3:4]
    for c in range(8):
        cs = slice(c * 128, (c + 1) * 128)
        x2s[:, cs] = xmid_ref[:, cs] + w1 * gbuf[buf, 0:tm, c, :] + w2 * gbuf[buf, tm:2 * tm, c, :]
    x2 = x2s[...]
    gate = _sigmoid(_mm(_rms(x2, gple_ref[...]), wpg_ref[...], mode))
    x3 = x2 + _mm(p_ref[...], wple_ref[...], mode) * gate
    x_ref[...] = x3
    y_ref[...] = _rms(x3, gfin_ref[...])


def _combine_ple(pos, xmid, info, p, wple, wpg, gple, gfin, y_sorted, *, mode, tm, slot0):
    n_tok = xmid.shape[0]
    full = lambda shape: pl.BlockSpec(shape, lambda i, pos: (0,) * len(shape))
    tok = lambda w: pl.BlockSpec((tm, w), lambda i, pos: (i, 0))
    grid_spec = pltpu.PrefetchScalarGridSpec(
        num_scalar_prefetch=1,
        grid=(n_tok // tm,),
        in_specs=[tok(D), tok(128), tok(D_PLE), full((D_PLE, D)), full((D, D)), full((1, D)), full((1, D)),
                  pl.BlockSpec(memory_space=pl.ANY)],
        out_specs=[tok(D), tok(D)],
        scratch_shapes=[pltpu.VMEM((2, 2 * tm, 8, 128), F32), pltpu.VMEM((tm, D), F32),
                        pltpu.SemaphoreType.DMA((2,))],
    )
    return pl.pallas_call(
        functools.partial(_combine_ple_kernel, mode=mode, tm=tm, slot0=slot0),
        grid_spec=grid_spec,
        out_shape=[jax.ShapeDtypeStruct((n_tok, D), F32), jax.ShapeDtypeStruct((n_tok, D), F32)],
        compiler_params=_cparams(("arbitrary",)),
        name="combine_ple_" + mode,
    )(pos, xmid, info, p, wple, wpg, gple, gfin, y_sorted)


def _sample_in_kernel(x_ref, gmix_ref, win_ref, convw_ref, s0_ref, s1_ref, lng_ref, lnb_ref,
                      ws0_ref, bs0_ref, bf_ref, gout_ref,
                      mab_ref, q_ref, k_ref, v_ref, logf_ref, z_ref, vn_ref):
    h = _rms(x_ref[...], gmix_ref[...])
    h_hi, h_lo = _split(h)

    def proj(lo, hi):
        w_hi, w_lo = _split(win_ref[:, lo:hi])
        return _dot(h_hi, w_hi) + _dot(h_lo, w_hi) + _dot(h_hi, w_lo)

    pa = proj(0, 3 * W_A)
    z = pa[:, W_A:2 * W_A] * pa[:, 2 * W_A:3 * W_A]
    cw = convw_ref[...]
    y_a = pa[:, 0:W_A] * (cw[0:1, :] * s0_ref[...] + cw[1:2, :] * s1_ref[...] + cw[2:3, :] * z)
    z_ref[...] = z
    gout = gout_ref[...]
    mab_ref[:, 0:W_A] = _rms(y_a, gout[:, 0:W_A])

    o = 3 * W_A
    pb = proj(o, o + 2 * W_B)
    v_b = pb[:, W_B:2 * W_B]
    mu = jnp.mean(v_b, axis=-1, keepdims=True)
    vc = v_b - mu
    var = jnp.mean(vc * vc, axis=-1, keepdims=True)
    vn = vc * lax.rsqrt(var + EPS) * lng_ref[...] + lnb_ref[...]
    vn_ref[...] = vn
    y_b = pb[:, 0:W_B] * (ws0_ref[...] * vn + bs0_ref[...])
    mab_ref[:, W_A:W_A + W_B] = _rms(y_b, gout[:, W_A:W_A + W_B])

    o = 3 * W_A + 2 * W_B
    q_ref[...] = proj(o, o + W_C) * (HD ** -0.5)
    k_ref[...] = proj(o + W_C, o + 2 * W_C)
    v_ref[...] = proj(o + 2 * W_C, o + 3 * W_C)
    f = proj(o + 3 * W_C, D_IN_PAD)
    logf_ref[...] = _log_sigmoid(f + bf_ref[...])


def _sample_in(x, gmix, win_pad, convw, s0, s1, lng, lnb, ws0, bs0, bf_pad, gout):
    shapes = [(NS, W_A + W_B), (NS, W_C), (NS, W_C), (NS, W_C), (NS, 128), (NS, W_A), (NS, W_B)]
    return pl.pallas_call(
        _sample_in_kernel,
        out_shape=[jax.ShapeDtypeStruct(s, F32) for s in shapes],
        compiler_params=pltpu.CompilerParams(vmem_limit_bytes=VMEM_LIMIT),
        name="sample_in",
    )(x, gmix, win_pad, convw, s0, s1, lng, lnb, ws0, bs0, bf_pad, gout)


def _sample_attn_kernel(pt_ref, qb_ref, q8_ref, kn_ref, vnt_ref, lfn_ref, gct_ref, *rest):
    k_refs = rest[0:PP]
    v_refs = rest[PP:2 * PP]
    lf_refs = rest[2 * PP:3 * PP]
    o_ref = rest[3 * PP]
    m_s, l_s, acc_s, tot_s = rest[3 * PP + 1:]
    j = pl.program_id(1)

    @pl.when(j == 0)
    def _():
        m_s[...] = jnp.full_like(m_s, NEG)
        l_s[...] = jnp.zeros_like(l_s)
        acc_s[...] = jnp.zeros_like(acc_s)
        tot_s[...] = jnp.zeros_like(tot_s)

    cn = lfn_ref[...]
    lane = lax.broadcasted_iota(I32, (H_C, PAGE), 1)

    tot = tot_s[...]
    scores = []
    for pp in reversed(range(PP)):
        lf = lf_refs[pp][...]
        inc = lf
        sh = 1
        while sh < PAGE:
            inc = inc + jnp.where(lane >= sh, pltpu.roll(inc, sh, axis=1), 0.0)
            sh *= 2
        page_tot = inc[:, PAGE - 1:PAGE]
        bias = (page_tot - inc) + tot + cn
        tot = tot + page_tot
        rows = [jnp.sum(k_refs[pp][h] * qb_ref[h], axis=0, keepdims=True) for h in range(H_C)]
        scores.append((pp, jnp.concatenate(rows, axis=0) + bias))
    tot_s[...] = tot

    m_old = m_s[...]
    m = m_old
    for _, s in scores:
        m = jnp.maximum(m, jnp.max(s, axis=-1, keepdims=True))
    a = jnp.exp(m_old - m)
    l = a * l_s[...]
    probs = []
    for pp, s in scores:
        p = jnp.exp(s - m)
        l = l + jnp.sum(p, axis=-1, keepdims=True)
        probs.append((pp, p))
    m_s[...] = m
    l_s[...] = l
    for h in range(H_C):
        acc_h = acc_s[h] * a[h:h + 1, :]
        for pp, p in probs:
            acc_h = acc_h + v_refs[pp][h] * p[h:h + 1, :]
        acc_s[h] = acc_h

    @pl.when(j == NPG - 1)
    def _():
        s_new = jnp.sum(q8_ref[...] * kn_ref[...], axis=-1, keepdims=True)
        mn = jnp.maximum(m, s_new)
        a2 = jnp.exp(m - mn)
        pn = jnp.exp(s_new - mn)
        lt = a2 * l + pn
        outs = []
        ms = jnp.zeros((1, 1), F32)
        for h in range(H_C):
            hs = slice(h, h + 1)
            o_h = (a2[hs, :] * jnp.sum(acc_s[h], axis=1, keepdims=True) + pn[hs, :] * vnt_ref[h]) / lt[hs, :]
            ms = ms + jnp.sum(o_h * o_h, axis=0, keepdims=True)
            outs.append(o_h)
        scale = lax.rsqrt(ms / W_C + EPS)
        for h in range(H_C):
            o_ref[h] = outs[h] * scale * gct_ref[h]


def _sample_attn(pt_flat, qb, q8, kn8, vnt, lfn3, gct, ckt, cvt, clft, layer):
    def page_map(nd):
        return lambda pp: (lambda n, j, pt: (layer, pt[n * N_PAGES + (NPG - 1 - j) * PP + pp]) + (0,) * nd)

    head = pl.BlockSpec((None, H_C, HD), lambda n, j, pt: (n, 0, 0))
    col = pl.BlockSpec((None, H_C, HD, 1), lambda n, j, pt: (n, 0, 0, 0))
    in_specs = [pl.BlockSpec((None, H_C, HD, PAGE), lambda n, j, pt: (n, 0, 0, 0)),
                head, head, col,
                pl.BlockSpec((None, H_C, 1), lambda n, j, pt: (n, 0, 0)),
                pl.BlockSpec((H_C, HD, 1), lambda n, j, pt: (0, 0, 0))]
    in_specs += [pl.BlockSpec((None, None, H_C, HD, PAGE), page_map(3)(pp)) for pp in range(PP)]
    in_specs += [pl.BlockSpec((None, None, H_C, HD, PAGE), page_map(3)(pp)) for pp in range(PP)]
    in_specs += [pl.BlockSpec((None, None, H_C, PAGE), page_map(2)(pp)) for pp in range(PP)]
    grid_spec = pltpu.PrefetchScalarGridSpec(
        num_scalar_prefetch=1,
        grid=(NS, NPG),
        in_specs=in_specs,
        out_specs=col,
        scratch_shapes=[pltpu.VMEM((H_C, 1), F32), pltpu.VMEM((H_C, 1), F32),
                        pltpu.VMEM((H_C, HD, PAGE), F32), pltpu.VMEM((H_C, 1), F32)],
    )
    return pl.pallas_call(
        _sample_attn_kernel,
        grid_spec=grid_spec,
        out_shape=jax.ShapeDtypeStruct((NS, H_C, HD, 1), F32),
        compiler_params=_cparams(("arbitrary", "arbitrary")),
        name="sample_attn",
    )(pt_flat, qb, q8, kn8, vnt, lfn3, gct, *([ckt] * PP), *([cvt] * PP), *([clft] * PP))


def _schedule(cnt_f, info_all):
    cnt = cnt_f[0, :NE].astype(I32)
    ntile = (cnt + TMX - 1) // TMX
    tcum = jnp.cumsum(ntile).astype(I32)
    tbase = tcum - ntile
    t = jnp.arange(NT, dtype=I32)
    tile_e = jnp.minimum(jnp.sum((tcum[None, :] <= t[:, None]).astype(I32), axis=1), NE - 1)
    sel = tile_e[:, None] == jnp.arange(NE, dtype=I32)[None, :]
    within = t - jnp.sum(jnp.where(sel, tbase[None, :], 0), axis=1)
    tile_cnt = jnp.clip(jnp.sum(jnp.where(sel, cnt[None, :], 0), axis=1) - within * TMX, 0, TMX).astype(I32)
    eid = info_all[:, 0:2].astype(I32)
    rank = info_all[:, 4:6].astype(I32)
    base = jnp.sum(jnp.where(eid[:, :, None] == jnp.arange(NE, dtype=I32)[None, None, :],
                             (tbase * TMX)[None, None, :], 0), axis=-1)
    pos = (base + rank).reshape(-1).astype(I32)
    return tile_e, tile_cnt, pos


def kernel(x_prompt, x_sample, p_prompt, p_sample, cache_k, cache_v, cache_logf, state_conv, page_table, g_mix, w_in, conv_w, ln_g, ln_b, w_s, b_s, b_f, g_out, w_o, g_ffn, w_grp, b_grp, w_rt, b_rt, w_gate, w_up, w_down, g_ple, w_ple, w_ple_gate, g_final):
    n_pool = cache_k.shape[1]
    xp = x_prompt.reshape(T, D)
    xs = x_sample.reshape(NS, D)
    pt_flat = page_table.reshape(-1).astype(I32)
    del n_pool
    ckt = jnp.transpose(cache_k, (0, 1, 3, 4, 2))
    cvt = jnp.transpose(cache_v, (0, 1, 3, 4, 2))
    clft = jnp.swapaxes(cache_logf, 2, 3)
    zero_cnt = jnp.zeros((1, 128), F32)
    wg_all = w_gate.reshape(DEPTH * NE, D, DE)
    wu_all = w_up.reshape(DEPTH * NE, D, DE)
    wd_all = w_down.reshape(DEPTH * NE, DE, D)
    gfin = g_final.reshape(1, D)

    outs = {k: [] for k in ("kp", "vp", "lfp", "cvp", "ks", "vs", "lfs", "cvs", "chv")}
    yp = ys = None
    for i in range(DEPTH):
        gmix = g_mix[i].reshape(1, D)
        gout = g_out[i].reshape(1, D)
        gffn = g_ffn[i].reshape(1, D)
        gple = g_ple[i].reshape(1, D)
        win_pad = jnp.pad(w_in[i], ((0, 0), (0, D_IN_PAD - D_IN)))
        win_bf = win_pad.astype(BF16)
        bf_pad = jnp.pad(b_f[i], (0, 128 - H_C)).reshape(1, 128)
        lng = ln_g[i].reshape(1, W_B)
        lnb = ln_b[i].reshape(1, W_B)
        bse = jnp.repeat(b_s[i].T, HD, axis=1)
        ws0 = jnp.repeat(w_s[i][:, 0, 0], HD).reshape(1, W_B)
        bs0 = jnp.repeat(b_s[i][:, 0], HD).reshape(1, W_B)
        wr = jnp.pad(jnp.concatenate([w_grp[i], w_rt[i].reshape(D, NE)], axis=1), ((0, 0), (0, 128 - NG - NE)))
        br = jnp.pad(jnp.concatenate([b_grp[i], b_rt[i].reshape(NE)]), (0, 128 - NG - NE)).reshape(1, 128)

        mab, qbf, kbf, vbf, k_p, v_p, lf_p, c_p, ctail = _prompt_in(
            xp, gmix, win_bf, conv_w[i], lng, lnb, w_s[i], bse, bf_pad, gout)
        ct = jnp.swapaxes(c_p.reshape(NB, L, H_C), 1, 2)
        mc = _prompt_attn(qbf, kbf, vbf, c_p, ct, gout)
        xmid_p, h2_p, info_p, cnt_p = _out_route(mab, mc, xp, w_o[i].astype(BF16), gffn, wr.astype(BF16), br,
                                                 zero_cnt, mode="bf16", tm=TM_C)

        mab_s, q_s, k_s, v_s, lf_s128, z_s, vn_s = _sample_in(
            xs, gmix, win_pad, conv_w[i], state_conv[i, :, 0], state_conv[i, :, 1], lng, lnb, ws0, bs0,
            bf_pad, gout)
        lf_s = lf_s128[:, :H_C]
        q8 = q_s.reshape(NS, H_C, HD)
        mc_s = _sample_attn(pt_flat, jnp.broadcast_to(q8[..., None], (NS, H_C, HD, PAGE)), q8,
                            k_s.reshape(NS, H_C, HD), v_s.reshape(NS, H_C, HD, 1), lf_s.reshape(NS, H_C, 1),
                            gout[:, W_A + W_B:].reshape(H_C, HD, 1), ckt, cvt, clft, i)
        xmid_s, h2_s, info_s, cnt_all = _out_route(mab_s, mc_s.reshape(NS, W_C), xs, w_o[i], gffn, wr, br,
                                                   cnt_p, mode="x3", tm=NS)

        tile_e, tile_cnt, pos = _schedule(cnt_all, jnp.concatenate([info_p[:, :8], info_s[:, :8]], axis=0))
        xs_sorted = _dispatch(pos, h2_p, h2_s)
        y_sorted = _experts(tile_e, tile_cnt, xs_sorted, wg_all, wu_all, wd_all, i)

        xp, yp = _combine_ple(pos, xmid_p, info_p, p_prompt[i].reshape(T, D_PLE), w_ple[i].astype(BF16),
                              w_ple_gate[i].astype(BF16), gple, gfin, y_sorted, mode="bf16", tm=TM_F, slot0=0)
        xs, ys = _combine_ple(pos, xmid_s, info_s, p_sample[i].reshape(NS, D_PLE), w_ple[i],
                              w_ple_gate[i], gple, gfin, y_sorted, mode="x3", tm=NS, slot0=2 * T)

        outs["kp"].append(k_p.reshape(NB, L, H_C, HD))
        outs["vp"].append(v_p.reshape(NB, L, H_C, HD))
        outs["lfp"].append(lf_p.reshape(NB, L, H_C))
        outs["cvp"].append(ctail)
        outs["ks"].append(k_s.reshape(NS, 1, H_C, HD))
        outs["vs"].append(v_s.reshape(NS, 1, H_C, HD))
        outs["lfs"].append(lf_s.reshape(NS, 1, H_C))
        outs["cvs"].append(jnp.stack([state_conv[i, :, 1], z_s], axis=1))
        outs["chv"].append(vn_s.reshape(NS, 1, W_B))

    st = lambda k: jnp.stack(outs[k])
    return (yp.reshape(NB, L, D), ys.reshape(NS, 1, D), st("kp"), st("vp"), st("lfp"), st("cvp"),
            st("ks"), st("vs"), st("lfs"), st("cvs"), st("chv"))
```

```python
import functools

import jax
import jax.numpy as jnp
from jax import lax
from jax.experimental import pallas as pl
from jax.experimental.pallas import tpu as pltpu

F32 = jnp.float32
BF16 = jnp.bfloat16
I32 = jnp.int32

D = 1024
NB = 8
L = 2048
T = NB * L
DEPTH = 2
NS = 32
PAGE = 128
N_PAGES = 64
H_C = 8
HD = 64
W_A = 256
W_B = 256
W_C = 512
D_IN = 2824
D_IN_PAD = 2944
NG = 4
EPG = 8
NE = NG * EPG
DE = 512
D_PLE = 256
CHUNK = 128
EPS = 1e-6
NEG = -jnp.inf

NCH = D // 128
T_ALL = T + NS
S_SLOTS = 2 * T_ALL
TMX = 256
NT = S_SLOTS // TMX + NE
S_PAD = NT * TMX
N_SPARE = NT - (S_SLOTS + TMX - 1) // TMX

TM_A = 512
TQ = 512
TM_C = 512
TM_F = 512
PP = 16
NPG = N_PAGES // PP

VMEM_LIMIT = 56 * 1024 * 1024


def _cparams(sem):
    return pltpu.CompilerParams(dimension_semantics=sem, vmem_limit_bytes=VMEM_LIMIT)


def _split(a):
    hi = a.astype(BF16)
    lo = (a - hi.astype(F32)).astype(BF16)
    return hi, lo


def _dot(a, b):
    return jnp.dot(a, b, preferred_element_type=F32)


def _dot_nt(a, b):
    return lax.dot_general(a, b, (((1,), (1,)), ((), ())), preferred_element_type=F32)


def _mm(a, b, mode):
    if mode == "bf16":
        return _dot(a.astype(BF16), b.astype(BF16))
    a_hi, a_lo = _split(a)
    b_hi, b_lo = _split(b)
    return _dot(a_hi, b_hi) + _dot(a_lo, b_hi) + _dot(a_hi, b_lo)


def _rms(x, g):
    return x * lax.rsqrt(jnp.mean(x * x, axis=-1, keepdims=True) + EPS) * g


def _log_sigmoid(x):
    return jnp.minimum(x, 0.0) - jnp.log1p(jnp.exp(-jnp.abs(x)))


def _sigmoid(x):
    return 1.0 / (1.0 + jnp.exp(-x))


def _prompt_in_kernel(x_ref, gmix_ref, win_ref, convw_ref, lng_ref, lnb_ref, ws_ref, bse_ref,
                      bf_ref, gout_ref,
                      mab_ref, qbf_ref, kbf_ref, vbf_ref, k_ref, v_ref, logf_ref, c_ref, ctail_ref,
                      zbuf, ccar):
    tm = TM_A
    i = pl.program_id(0)

    @pl.when(i % (L // tm) == 0)
    def _():
        zbuf[0:8, :] = jnp.zeros((8, W_A), F32)
        ccar[...] = jnp.zeros_like(ccar)

    hb = _rms(x_ref[...], gmix_ref[...]).astype(BF16)

    def proj(lo, hi):
        return _dot(hb, win_ref[:, lo:hi])

    pa = proj(0, 3 * W_A)
    z = pa[:, W_A:2 * W_A] * pa[:, 2 * W_A:3 * W_A]
    zbuf[8:8 + tm, :] = z
    z1 = zbuf[7:7 + tm, :]
    z2 = zbuf[6:6 + tm, :]
    cw = convw_ref[...]
    y_a = pa[:, 0:W_A] * (cw[0:1, :] * z2 + cw[1:2, :] * z1 + cw[2:3, :] * z)
    tail = z[tm - 2:tm, :]
    ctail_ref[...] = tail
    zbuf[6:8, :] = tail
    gout = gout_ref[...]
    mab_ref[:, 0:W_A] = _rms(y_a, gout[:, 0:W_A]).astype(BF16)

    o = 3 * W_A
    pb = proj(o, o + 2 * W_B)
    u_b = pb[:, 0:W_B]
    v_b = pb[:, W_B:2 * W_B]
    mu = jnp.mean(v_b, axis=-1, keepdims=True)
    vc = v_b - mu
    var = jnp.mean(vc * vc, axis=-1, keepdims=True)
    vn = vc * lax.rsqrt(var + EPS) * lng_ref[...] + lnb_ref[...]
    vnb = vn.astype(BF16)
    r_i = lax.broadcasted_iota(I32, (CHUNK, CHUNK), 0)
    c_i = lax.broadcasted_iota(I32, (CHUNK, CHUNK), 1)
    lane_head = lax.broadcasted_iota(I32, (CHUNK, W_B), 1) // HD
    ws_t = [jnp.where(r_i >= c_i, ws_ref[hh], 0.0).astype(BF16) for hh in range(4)]
    parts = []
    for cidx in range(tm // CHUNK):
        vchunk = vnb[cidx * CHUNK:(cidx + 1) * CHUNK, :]
        sc = jnp.zeros((CHUNK, W_B), F32)
        for hh in range(4):
            sc = jnp.where(lane_head == hh, _dot(ws_t[hh], vchunk), sc)
        parts.append(sc + bse_ref[...])
    s = jnp.concatenate(parts, axis=0)
    y_b = u_b * s
    mab_ref[:, W_A:W_A + W_B] = _rms(y_b, gout[:, W_A:W_A + W_B]).astype(BF16)

    o = 3 * W_A + 2 * W_B
    q = proj(o, o + W_C)
    k = proj(o + W_C, o + 2 * W_C)
    v = proj(o + 2 * W_C, o + 3 * W_C)
    f = proj(o + 3 * W_C, D_IN_PAD)
    qbf_ref[...] = (q * (HD ** -0.5)).astype(BF16)
    k_ref[...] = k
    v_ref[...] = v
    kbf_ref[...] = k.astype(BF16)
    vbf_ref[...] = v.astype(BF16)
    lf = _log_sigmoid(f + bf_ref[...])
    logf_ref[...] = lf[:, 0:H_C]

    p1 = lf.astype(BF16)
    r1 = lf - p1.astype(F32)
    p2 = r1.astype(BF16)
    p3 = (r1 - p2.astype(F32)).astype(BF16)
    tr = lax.broadcasted_iota(I32, (tm, tm), 0) >= lax.broadcasted_iota(I32, (tm, tm), 1)
    trb = jnp.where(tr, 1.0, 0.0).astype(BF16)
    cs = _dot(trb, p1) + _dot(trb, p2) + _dot(trb, p3) + ccar[...]
    c_ref[...] = cs[:, 0:H_C]
    ccar[...] = cs[tm - 1:tm, :]


def _prompt_in(x, gmix, win_bf, convw, lng, lnb, ws, bse, bf_pad, gout):
    tm = TM_A
    n = T // tm
    full = lambda shape: pl.BlockSpec(shape, lambda i: (0,) * len(shape))
    tok = lambda w: pl.BlockSpec((tm, w), lambda i: (i, 0))
    return pl.pallas_call(
        _prompt_in_kernel,
        grid=(n,),
        in_specs=[tok(D), full((1, D)), full((D, D_IN_PAD)), full((3, W_A)), full((1, W_B)),
                  full((1, W_B)), full((4, CHUNK, CHUNK)), full((CHUNK, W_B)), full((1, 128)),
                  full((1, D))],
        out_specs=[tok(W_A + W_B), tok(W_C), tok(W_C), tok(W_C), tok(W_C), tok(W_C), tok(H_C), tok(H_C),
                   pl.BlockSpec((None, 2, W_A), lambda i: (i // (L // tm), 0, 0))],
        out_shape=[jax.ShapeDtypeStruct((T, W_A + W_B), BF16),
                   jax.ShapeDtypeStruct((T, W_C), BF16),
                   jax.ShapeDtypeStruct((T, W_C), BF16),
                   jax.ShapeDtypeStruct((T, W_C), BF16),
                   jax.ShapeDtypeStruct((T, W_C), F32),
                   jax.ShapeDtypeStruct((T, W_C), F32),
                   jax.ShapeDtypeStruct((T, H_C), F32),
                   jax.ShapeDtypeStruct((T, H_C), F32),
                   jax.ShapeDtypeStruct((NB, 2, W_A), F32)],
        scratch_shapes=[pltpu.VMEM((tm + 8, W_A), F32), pltpu.VMEM((1, 128), F32)],
        compiler_params=_cparams(("arbitrary",)),
        name="prompt_in",
    )(x, gmix, win_bf, convw, lng, lnb, ws, bse, bf_pad, gout)


def _prompt_attn_kernel(q_ref, k_ref, v_ref, ct_ref, gout_ref, o_ref, yc):
    tq = TQ
    qi = pl.program_id(1)
    left = lax.broadcasted_iota(I32, (1, 128), 1) < HD
    row = lax.broadcasted_iota(I32, (tq, tq), 0)
    col = lax.broadcasted_iota(I32, (tq, tq), 1)
    zero = jnp.zeros((), BF16)

    for hp in range(H_C // 2):
        ls = slice(hp * 128, (hp + 1) * 128)
        q2 = q_ref[:, ls]
        q_a = jnp.where(left, q2, zero)
        q_b = jnp.where(left, zero, q2)
        h_a, h_b = 2 * hp, 2 * hp + 1

        def block(ki, carry, masked):
            m_a, l_a, m_b, l_b, acc = carry
            ks = pl.multiple_of(ki * tq, tq)
            k2 = k_ref[pl.ds(ks, tq), ls]
            v2 = v_ref[pl.ds(ks, tq), ls]

            def one(qh, h, m, l):
                s = _dot_nt(qh, k2) - ct_ref[h:h + 1, pl.ds(ks, tq)]
                if masked:
                    s = jnp.where(col <= row, s, NEG)
                mn = jnp.maximum(m, jnp.max(s, axis=-1, keepdims=True))
                a = jnp.exp(m - mn)
                p = jnp.exp(s - mn)
                ln = a * l + jnp.sum(p, axis=-1, keepdims=True)
                return mn, ln, a, _dot(p.astype(BF16), v2)

            m_a, l_a, a_a, pv_a = one(q_a, h_a, m_a, l_a)
            m_b, l_b, a_b, pv_b = one(q_b, h_b, m_b, l_b)
            acc = acc * jnp.where(left, a_a, a_b) + jnp.where(left, pv_a, pv_b)
            return m_a, l_a, m_b, l_b, acc

        init = (jnp.full((tq, 1), NEG, F32), jnp.zeros((tq, 1), F32),
                jnp.full((tq, 1), NEG, F32), jnp.zeros((tq, 1), F32),
                jnp.zeros((tq, 128), F32))
        carry = lax.fori_loop(0, qi, lambda ki, c: block(ki, c, False), init)
        m_a, l_a, m_b, l_b, acc = block(qi, carry, True)
        yc[:, ls] = acc / jnp.where(left, l_a, l_b)

    o_ref[...] = _rms(yc[...], gout_ref[:, W_A + W_B:]).astype(BF16)


def _prompt_attn(qbf, kbf, vbf, ct, gout):
    nq = L // TQ
    return pl.pallas_call(
        _prompt_attn_kernel,
        grid=(NB, nq),
        in_specs=[pl.BlockSpec((TQ, W_C), lambda n, qi: (n * nq + qi, 0)),
                  pl.BlockSpec((L, W_C), lambda n, qi: (n, 0)),
                  pl.BlockSpec((L, W_C), lambda n, qi: (n, 0)),
                  pl.BlockSpec((None, H_C, L), lambda n, qi: (n, 0, 0)),
                  pl.BlockSpec((1, D), lambda n, qi: (0, 0))],
        out_specs=pl.BlockSpec((TQ, W_C), lambda n, qi: (n * nq + qi, 0)),
        out_shape=jax.ShapeDtypeStruct((T, W_C), BF16),
        scratch_shapes=[pltpu.VMEM((TQ, W_C), F32)],
        compiler_params=_cparams(("arbitrary", "arbitrary")),
        name="prompt_attn",
    )(qbf, kbf, vbf, ct, gout)


def _route(logits):
    lane = lax.broadcasted_iota(I32, logits.shape, 1).astype(F32)
    big = jnp.float32(1e9)
    gl = jnp.where(lane < NG, logits, NEG)
    gmax = jnp.max(gl, axis=-1, keepdims=True)
    p_g = 1.0 / jnp.sum(jnp.exp(gl - gmax), axis=-1, keepdims=True)
    g_idx = jnp.min(jnp.where(gl == gmax, lane, big), axis=-1, keepdims=True)
    lo = NG + EPG * g_idx
    el = jnp.where((lane >= lo) & (lane < lo + EPG), logits, NEG)
    m1 = jnp.max(el, axis=-1, keepdims=True)
    i1 = jnp.min(jnp.where(el == m1, lane, big), axis=-1, keepdims=True)
    zsum = jnp.sum(jnp.exp(el - m1), axis=-1, keepdims=True)
    el2 = jnp.where(lane == i1, NEG, el)
    m2 = jnp.max(el2, axis=-1, keepdims=True)
    i2 = jnp.min(jnp.where(el2 == m2, lane, big), axis=-1, keepdims=True)
    p1 = 1.0 / zsum
    p2 = jnp.exp(m2 - m1) / zsum
    den = p1 + p2
    w1 = p_g * p1 / den
    w2 = p_g * p2 / den
    e1 = i1 - NG
    e2 = i2 - NG
    info = jnp.where(lane == 0, e1,
                     jnp.where(lane == 1, e2,
                               jnp.where(lane == 2, w1, jnp.where(lane == 3, w2, 0.0))))
    return info, e1, e2


def _out_route_kernel(mab_ref, mc_ref, x_ref, wo_ref, gffn_ref, wr_ref, br_ref, cin_ref,
                      xmid_ref, h2_ref, info_ref, cnt_ref, carry, *, mode, tm):
    @pl.when(pl.program_id(0) == 0)
    def _():
        carry[...] = cin_ref[...]

    upd = _mm(mab_ref[...], wo_ref[0:W_A + W_B, :], mode) + _mm(mc_ref[...], wo_ref[W_A + W_B:, :], mode)
    xm = x_ref[...] + upd
    xmid_ref[...] = xm
    h2 = _rms(xm, gffn_ref[...])
    for c in range(NCH):
        h2_ref[pl.ds(c, tm, stride=NCH), :] = h2[:, c * 128:(c + 1) * 128]
    logits = _mm(h2, wr_ref[...], mode) + br_ref[...]
    info, e1, e2 = _route(logits)

    lane = lax.broadcasted_iota(I32, (tm, 128), 1).astype(F32)
    hit1 = lane == e1
    hit2 = lane == e2
    both = jnp.where(hit1, 1.0, 0.0) + jnp.where(hit2, 1.0, 0.0)
    earlier = lax.broadcasted_iota(I32, (tm, tm), 0) > lax.broadcasted_iota(I32, (tm, tm), 1)
    before = _dot(jnp.where(earlier, 1.0, 0.0).astype(BF16), both.astype(BF16)) + carry[...]
    r1 = jnp.sum(jnp.where(hit1, before, 0.0), axis=-1, keepdims=True)
    r2 = jnp.sum(jnp.where(hit2, before, 0.0), axis=-1, keepdims=True)
    info_ref[...] = jnp.where(lane == 4, r1, jnp.where(lane == 5, r2, info))
    total = carry[...] + jnp.sum(both, axis=0, keepdims=True)
    carry[...] = total
    cnt_ref[...] = total


def _out_route(mab, mc, x, wo, gffn, wr, br, cnt_in, *, mode, tm):
    n_tok = x.shape[0]
    full = lambda shape: pl.BlockSpec(shape, lambda i: (0,) * len(shape))
    tok = lambda w: pl.BlockSpec((tm, w), lambda i: (i, 0))
    return pl.pallas_call(
        functools.partial(_out_route_kernel, mode=mode, tm=tm),
        grid=(n_tok // tm,),
        in_specs=[tok(W_A + W_B), tok(W_C), tok(D), full((D, D)), full((1, D)), full((D, 128)),
                  full((1, 128)), full((1, 128))],
        out_specs=[tok(D), pl.BlockSpec((tm * NCH, 128), lambda i: (i, 0)), tok(128), full((1, 128))],
        out_shape=[jax.ShapeDtypeStruct((n_tok, D), F32),
                   jax.ShapeDtypeStruct((n_tok * NCH, 128), F32),
                   jax.ShapeDtypeStruct((n_tok, 128), F32),
                   jax.ShapeDtypeStruct((1, 128), F32)],
        scratch_shapes=[pltpu.VMEM((1, 128), F32)],
        compiler_params=_cparams(("arbitrary",)),
        name="out_route_" + mode,
    )(mab, mc, x, wo, gffn, wr, br, cnt_in)


TM_D = 512
DISPATCH_UNROLL = 4


def _dispatch_kernel(pos_ref, zrow_ref, h2p_ref, h2s_ref, xs_ref, zbuf, sem, zsem):
    i = pl.program_id(0)
    last = i == pl.num_programs(0) - 1

    @pl.when(i == 0)
    def _():
        zbuf[...] = jnp.zeros_like(zbuf)

        def zero_copy(e):
            return pltpu.make_async_copy(zbuf, xs_ref.at[pl.ds(zrow_ref[e], TMX)], zsem)
        for e in range(NE + N_SPARE):
            pl.when(zrow_ref[e] >= 0)(lambda e=e: zero_copy(e).start())
        for e in range(NE + N_SPARE):
            pl.when(zrow_ref[e] >= 0)(lambda e=e: zero_copy(e).wait())

    def copy_token(src_ref, r, slot):
        for k in range(2):
            pltpu.make_async_copy(src_ref.at[r], xs_ref.at[pos_ref[slot + k]], sem).start()

    def wait_rows(n):
        pltpu.make_async_copy(xs_ref.at[pl.ds(0, n)], xs_ref.at[pl.ds(0, n)], sem).wait()

    def body(j, c):
        for u in range(DISPATCH_UNROLL):
            r = j * DISPATCH_UNROLL + u
            copy_token(h2p_ref, r, 2 * (i * TM_D + r))
        return c
    lax.fori_loop(0, TM_D // DISPATCH_UNROLL, body, 0)

    @pl.when(last)
    def _():
        def sample_body(j, c):
            copy_token(h2s_ref, j, 2 * (T + j))
            return c
        lax.fori_loop(0, NS, sample_body, 0)
        wait_rows(2 * NS)

    wait_rows(2 * TM_D)


def _dispatch(pos, zrow, h2p, h2s):
    grid_spec = pltpu.PrefetchScalarGridSpec(
        num_scalar_prefetch=2,
        grid=(T // TM_D,),
        in_specs=[pl.BlockSpec((TM_D, NCH, 128), lambda i, pos, zrow: (i, 0, 0)),
                  pl.BlockSpec((NS, NCH, 128), lambda i, pos, zrow: (0, 0, 0))],
        out_specs=pl.BlockSpec(memory_space=pl.ANY),
        scratch_shapes=[pltpu.VMEM((TMX, NCH, 128), F32), pltpu.SemaphoreType.DMA(()),
                        pltpu.SemaphoreType.DMA(())],
    )
    return pl.pallas_call(
        _dispatch_kernel,
        grid_spec=grid_spec,
        out_shape=jax.ShapeDtypeStruct((S_PAD, NCH, 128), F32),
        compiler_params=_cparams(("arbitrary",)),
        name="dispatch",
    )(pos, zrow, h2p.reshape(T, NCH, 128), h2s.reshape(NS, NCH, 128))


def _expert_kernel(te_ref, tc_ref, xs_ref, wg_ref, wu_ref, wd_ref, y_ref, wg_b, wu_b, wd_b):
    t = pl.program_id(0)
    prev = te_ref[jnp.maximum(t - 1, 0)]

    @pl.when((t == 0) | (te_ref[t] != prev))
    def _():
        wg_b[...] = wg_ref[...].astype(BF16)
        wu_b[...] = wu_ref[...].astype(BF16)
        wd_b[...] = wd_ref[...].astype(BF16)

    @pl.when(tc_ref[t] > 0)
    def _():
        xb = jnp.concatenate([xs_ref[pl.ds(c, TMX, stride=NCH), :] for c in range(NCH)], axis=1).astype(BF16)
        a = _dot(xb, wg_b[...])
        b = _dot(xb, wu_b[...])
        act = (a * _sigmoid(a) * b).astype(BF16)
        y = _dot(act, wd_b[...])
        for c in range(NCH):
            y_ref[pl.ds(c, TMX, stride=NCH), :] = y[:, c * 128:(c + 1) * 128]

    @pl.when(tc_ref[t] == 0)
    def _():
        y_ref[...] = jnp.zeros_like(y_ref)


def _experts(tile_e, tile_cnt, xs_sorted, wg, wu, wd, layer):
    grid_spec = pltpu.PrefetchScalarGridSpec(
        num_scalar_prefetch=2,
        grid=(NT,),
        in_specs=[pl.BlockSpec((TMX * NCH, 128), lambda t, te, tc: (t, 0)),
                  pl.BlockSpec((None, D, DE), lambda t, te, tc: (layer * NE + te[t], 0, 0)),
                  pl.BlockSpec((None, D, DE), lambda t, te, tc: (layer * NE + te[t], 0, 0)),
                  pl.BlockSpec((None, DE, D), lambda t, te, tc: (layer * NE + te[t], 0, 0))],
        out_specs=pl.BlockSpec((TMX * NCH, 128), lambda t, te, tc: (t, 0)),
        scratch_shapes=[pltpu.VMEM((D, DE), BF16), pltpu.VMEM((D, DE), BF16), pltpu.VMEM((DE, D), BF16)],
    )
    y = pl.pallas_call(
        _expert_kernel,
        grid_spec=grid_spec,
        out_shape=jax.ShapeDtypeStruct((S_PAD * NCH, 128), F32),
        compiler_params=_cparams(("arbitrary",)),
        name="experts",
    )(tile_e, tile_cnt, xs_sorted.reshape(S_PAD * NCH, 128), wg, wu, wd)
    return y.reshape(S_PAD, NCH, 128)


def _combine_ple_kernel(pos_ref, xmid_ref, info_ref, p_ref, wple_ref, wpg_ref, gple_ref, gfin_ref, ys_ref,
                        x_ref, y_ref, gbuf, x2s, sem, *, mode, tm, slot0):
    i = pl.program_id(0)
    n = pl.num_programs(0)

    def start_row(tile, buf, j):
        s = slot0 + 2 * (tile * tm + j)
        for k in range(2):
            dst = gbuf.at[buf, pl.ds(pl.multiple_of((k * tm + j) * NCH, NCH), NCH), :]
            pltpu.make_async_copy(ys_ref.at[pos_ref[s + k]], dst, sem.at[buf]).start()

    @pl.when(i == 0)
    def _():
        def body(j, c):
            start_row(0, 0, j)
            return c
        lax.fori_loop(0, tm, body, 0)

    buf = i % 2

    def compute(prefetch):
        nxt = 0
        def issue_some(count):
            nonlocal nxt
            if prefetch:
                for j in range(nxt, min(nxt + count, tm)):
                    start_row(i + 1, 1 - buf, j)
                nxt = min(nxt + count, tm)

        pltpu.make_async_copy(gbuf.at[buf], gbuf.at[buf], sem.at[buf]).wait()
        info = info_ref[...]
        w1 = info[:, 2:3]
        w2 = info[:, 3:4]
        for c in range(8):
            cs = slice(c * 128, (c + 1) * 128)
            y1 = gbuf[buf, pl.ds(c, tm, stride=NCH), :]
            y2 = gbuf[buf, pl.ds(tm * NCH + c, tm, stride=NCH), :]
            x2s[:, cs] = xmid_ref[:, cs] + w1 * y1 + w2 * y2
            issue_some(tm // 16)
        x2 = x2s[...]
        gate = _sigmoid(_mm(_rms(x2, gple_ref[...]), wpg_ref[...], mode))
        issue_some(tm // 4)
        x3 = x2 + _mm(p_ref[...], wple_ref[...], mode) * gate
        issue_some(tm)
        x_ref[...] = x3
        y_ref[...] = _rms(x3, gfin_ref[...])

    @pl.when(i + 1 < n)
    def _():
        compute(True)

    @pl.when(i + 1 == n)
    def _():
        compute(False)


def _combine_ple(pos, xmid, info, p, wple, wpg, gple, gfin, y_sorted, *, mode, tm, slot0):
    n_tok = xmid.shape[0]
    full = lambda shape: pl.BlockSpec(shape, lambda i, pos: (0,) * len(shape))
    tok = lambda w: pl.BlockSpec((tm, w), lambda i, pos: (i, 0))
    grid_spec = pltpu.PrefetchScalarGridSpec(
        num_scalar_prefetch=1,
        grid=(n_tok // tm,),
        in_specs=[tok(D), tok(128), tok(D_PLE), full((D_PLE, D)), full((D, D)), full((1, D)), full((1, D)),
                  pl.BlockSpec(memory_space=pl.ANY)],
        out_specs=[tok(D), tok(D)],
        scratch_shapes=[pltpu.VMEM((2, 2 * tm * NCH, 128), F32), pltpu.VMEM((tm, D), F32),
                        pltpu.SemaphoreType.DMA((2,))],
    )
    return pl.pallas_call(
        functools.partial(_combine_ple_kernel, mode=mode, tm=tm, slot0=slot0),
        grid_spec=grid_spec,
        out_shape=[jax.ShapeDtypeStruct((n_tok, D), F32), jax.ShapeDtypeStruct((n_tok, D), F32)],
        compiler_params=_cparams(("arbitrary",)),
        name="combine_ple_" + mode,
    )(pos, xmid, info, p, wple, wpg, gple, gfin, y_sorted)


def _sample_in_kernel(x_ref, gmix_ref, win_ref, convw_ref, s0_ref, s1_ref, lng_ref, lnb_ref,
                      ws0_ref, bs0_ref, bf_ref, gout_ref,
                      mab_ref, q_ref, k_ref, v_ref, logf_ref, z_ref, vn_ref):
    h = _rms(x_ref[...], gmix_ref[...])
    h_hi, h_lo = _split(h)

    def proj(lo, hi):
        w_hi, w_lo = _split(win_ref[:, lo:hi])
        return _dot(h_hi, w_hi) + _dot(h_lo, w_hi) + _dot(h_hi, w_lo)

    pa = proj(0, 3 * W_A)
    z = pa[:, W_A:2 * W_A] * pa[:, 2 * W_A:3 * W_A]
    cw = convw_ref[...]
    y_a = pa[:, 0:W_A] * (cw[0:1, :] * s0_ref[...] + cw[1:2, :] * s1_ref[...] + cw[2:3, :] * z)
    z_ref[...] = z
    gout = gout_ref[...]
    mab_ref[:, 0:W_A] = _rms(y_a, gout[:, 0:W_A])

    o = 3 * W_A
    pb = proj(o, o + 2 * W_B)
    v_b = pb[:, W_B:2 * W_B]
    mu = jnp.mean(v_b, axis=-1, keepdims=True)
    vc = v_b - mu
    var = jnp.mean(vc * vc, axis=-1, keepdims=True)
    vn = vc * lax.rsqrt(var + EPS) * lng_ref[...] + lnb_ref[...]
    vn_ref[...] = vn
    y_b = pb[:, 0:W_B] * (ws0_ref[...] * vn + bs0_ref[...])
    mab_ref[:, W_A:W_A + W_B] = _rms(y_b, gout[:, W_A:W_A + W_B])

    o = 3 * W_A + 2 * W_B
    q_ref[...] = proj(o, o + W_C) * (HD ** -0.5)
    k_ref[...] = proj(o + W_C, o + 2 * W_C)
    v_ref[...] = proj(o + 2 * W_C, o + 3 * W_C)
    f = proj(o + 3 * W_C, D_IN_PAD)
    logf_ref[...] = _log_sigmoid(f + bf_ref[...])


def _sample_in(x, gmix, win_pad, convw, s0, s1, lng, lnb, ws0, bs0, bf_pad, gout):
    shapes = [(NS, W_A + W_B), (NS, W_C), (NS, W_C), (NS, W_C), (NS, 128), (NS, W_A), (NS, W_B)]
    return pl.pallas_call(
        _sample_in_kernel,
        out_shape=[jax.ShapeDtypeStruct(s, F32) for s in shapes],
        compiler_params=pltpu.CompilerParams(vmem_limit_bytes=VMEM_LIMIT),
        name="sample_in",
    )(x, gmix, win_pad, convw, s0, s1, lng, lnb, ws0, bs0, bf_pad, gout)


def _sample_attn_kernel(pt_ref, qb_ref, q8_ref, kn_ref, vnt_ref, lfn_ref, gct_ref, *rest):
    k_refs = rest[0:PP]
    v_refs = rest[PP:2 * PP]
    lf_refs = rest[2 * PP:3 * PP]
    o_ref = rest[3 * PP]
    m_s, l_s, acc_s, tot_s = rest[3 * PP + 1:]
    j = pl.program_id(1)

    @pl.when(j == 0)
    def _():
        m_s[...] = jnp.full_like(m_s, NEG)
        l_s[...] = jnp.zeros_like(l_s)
        acc_s[...] = jnp.zeros_like(acc_s)
        tot_s[...] = jnp.zeros_like(tot_s)

    cn = lfn_ref[...]
    lane = lax.broadcasted_iota(I32, (H_C, PAGE), 1)

    tot = tot_s[...]
    scores = []
    for pp in reversed(range(PP)):
        lf = lf_refs[pp][...]
        inc = lf
        sh = 1
        while sh < PAGE:
            inc = inc + jnp.where(lane >= sh, pltpu.roll(inc, sh, axis=1), 0.0)
            sh *= 2
        page_tot = inc[:, PAGE - 1:PAGE]
        bias = (page_tot - inc) + tot + cn
        tot = tot + page_tot
        rows = [jnp.sum(k_refs[pp][h] * qb_ref[h], axis=0, keepdims=True) for h in range(H_C)]
        scores.append((pp, jnp.concatenate(rows, axis=0) + bias))
    tot_s[...] = tot

    m_old = m_s[...]
    m = m_old
    for _, s in scores:
        m = jnp.maximum(m, jnp.max(s, axis=-1, keepdims=True))
    a = jnp.exp(m_old - m)
    l = a * l_s[...]
    probs = []
    for pp, s in scores:
        p = jnp.exp(s - m)
        l = l + jnp.sum(p, axis=-1, keepdims=True)
        probs.append((pp, p))
    m_s[...] = m
    l_s[...] = l
    for h in range(H_C):
        acc_h = acc_s[h] * a[h:h + 1, :]
        for pp, p in probs:
            acc_h = acc_h + v_refs[pp][h] * p[h:h + 1, :]
        acc_s[h] = acc_h

    @pl.when(j == NPG - 1)
    def _():
        s_new = jnp.sum(q8_ref[...] * kn_ref[...], axis=-1, keepdims=True)
        mn = jnp.maximum(m, s_new)
        a2 = jnp.exp(m - mn)
        pn = jnp.exp(s_new - mn)
        lt = a2 * l + pn
        outs = []
        ms = jnp.zeros((1, 1), F32)
        for h in range(H_C):
            hs = slice(h, h + 1)
            o_h = (a2[hs, :] * jnp.sum(acc_s[h], axis=1, keepdims=True) + pn[hs, :] * vnt_ref[h]) / lt[hs, :]
            ms = ms + jnp.sum(o_h * o_h, axis=0, keepdims=True)
            outs.append(o_h)
        scale = lax.rsqrt(ms / W_C + EPS)
        for h in range(H_C):
            o_ref[h] = outs[h] * scale * gct_ref[h]


def _sample_attn(pt_flat, qb, q8, kn8, vnt, lfn3, gct, ckt, cvt, clft, layer):
    def page_map(nd):
        return lambda pp: (lambda n, j, pt: (layer, pt[n * N_PAGES + (NPG - 1 - j) * PP + pp]) + (0,) * nd)

    head = pl.BlockSpec((None, H_C, HD), lambda n, j, pt: (n, 0, 0))
    col = pl.BlockSpec((None, H_C, HD, 1), lambda n, j, pt: (n, 0, 0, 0))
    in_specs = [pl.BlockSpec((None, H_C, HD, PAGE), lambda n, j, pt: (n, 0, 0, 0)),
                head, head, col,
                pl.BlockSpec((None, H_C, 1), lambda n, j, pt: (n, 0, 0)),
                pl.BlockSpec((H_C, HD, 1), lambda n, j, pt: (0, 0, 0))]
    in_specs += [pl.BlockSpec((None, None, H_C, HD, PAGE), page_map(3)(pp)) for pp in range(PP)]
    in_specs += [pl.BlockSpec((None, None, H_C, HD, PAGE), page_map(3)(pp)) for pp in range(PP)]
    in_specs += [pl.BlockSpec((None, None, H_C, PAGE), page_map(2)(pp)) for pp in range(PP)]
    grid_spec = pltpu.PrefetchScalarGridSpec(
        num_scalar_prefetch=1,
        grid=(NS, NPG),
        in_specs=in_specs,
        out_specs=col,
        scratch_shapes=[pltpu.VMEM((H_C, 1), F32), pltpu.VMEM((H_C, 1), F32),
                        pltpu.VMEM((H_C, HD, PAGE), F32), pltpu.VMEM((H_C, 1), F32)],
    )
    return pl.pallas_call(
        _sample_attn_kernel,
        grid_spec=grid_spec,
        out_shape=jax.ShapeDtypeStruct((NS, H_C, HD, 1), F32),
        compiler_params=_cparams(("arbitrary", "arbitrary")),
        name="sample_attn",
    )(pt_flat, qb, q8, kn8, vnt, lfn3, gct, *([ckt] * PP), *([cvt] * PP), *([clft] * PP))


def _schedule(cnt_f, info_all):
    cnt = cnt_f[0, :NE].astype(I32)
    ntile = (cnt + TMX - 1) // TMX
    tcum = jnp.cumsum(ntile).astype(I32)
    tbase = tcum - ntile
    t = jnp.arange(NT, dtype=I32)
    tile_e = jnp.minimum(jnp.sum((tcum[None, :] <= t[:, None]).astype(I32), axis=1), NE - 1)
    sel = tile_e[:, None] == jnp.arange(NE, dtype=I32)[None, :]
    within = t - jnp.sum(jnp.where(sel, tbase[None, :], 0), axis=1)
    tile_cnt = jnp.clip(jnp.sum(jnp.where(sel, cnt[None, :], 0), axis=1) - within * TMX, 0, TMX).astype(I32)
    eid = info_all[:, 0:2].astype(I32)
    rank = info_all[:, 4:6].astype(I32)
    base = jnp.sum(jnp.where(eid[:, :, None] == jnp.arange(NE, dtype=I32)[None, None, :],
                             (tbase * TMX)[None, None, :], 0), axis=-1)
    pos = (base + rank).reshape(-1).astype(I32)
    last_tile = jnp.where(ntile > 0, tcum - 1, -1)
    spare = tcum[NE - 1] + jnp.arange(N_SPARE, dtype=I32)
    ztile = jnp.concatenate([last_tile, jnp.where(spare < NT, spare, -1)])
    zrow = jnp.where(ztile >= 0, ztile * TMX, -1).astype(I32)
    return tile_e, tile_cnt, pos, zrow


def kernel(x_prompt, x_sample, p_prompt, p_sample, cache_k, cache_v, cache_logf, state_conv, page_table, g_mix, w_in, conv_w, ln_g, ln_b, w_s, b_s, b_f, g_out, w_o, g_ffn, w_grp, b_grp, w_rt, b_rt, w_gate, w_up, w_down, g_ple, w_ple, w_ple_gate, g_final):
    n_pool = cache_k.shape[1]
    xp = x_prompt.reshape(T, D)
    xs = x_sample.reshape(NS, D)
    pt_flat = page_table.reshape(-1).astype(I32)
    del n_pool
    ckt = jnp.transpose(cache_k, (0, 1, 3, 4, 2))
    cvt = jnp.transpose(cache_v, (0, 1, 3, 4, 2))
    clft = jnp.swapaxes(cache_logf, 2, 3)
    zero_cnt = jnp.zeros((1, 128), F32)
    wg_all = w_gate.reshape(DEPTH * NE, D, DE)
    wu_all = w_up.reshape(DEPTH * NE, D, DE)
    wd_all = w_down.reshape(DEPTH * NE, DE, D)
    gfin = g_final.reshape(1, D)

    outs = {k: [] for k in ("kp", "vp", "lfp", "cvp", "ks", "vs", "lfs", "cvs", "chv")}
    yp = ys = None
    for i in range(DEPTH):
        gmix = g_mix[i].reshape(1, D)
        gout = g_out[i].reshape(1, D)
        gffn = g_ffn[i].reshape(1, D)
        gple = g_ple[i].reshape(1, D)
        win_pad = jnp.pad(w_in[i], ((0, 0), (0, D_IN_PAD - D_IN)))
        win_bf = win_pad.astype(BF16)
        bf_pad = jnp.pad(b_f[i], (0, 128 - H_C)).reshape(1, 128)
        lng = ln_g[i].reshape(1, W_B)
        lnb = ln_b[i].reshape(1, W_B)
        bse = jnp.repeat(b_s[i].T, HD, axis=1)
        ws0 = jnp.repeat(w_s[i][:, 0, 0], HD).reshape(1, W_B)
        bs0 = jnp.repeat(b_s[i][:, 0], HD).reshape(1, W_B)
        wr = jnp.pad(jnp.concatenate([w_grp[i], w_rt[i].reshape(D, NE)], axis=1), ((0, 0), (0, 128 - NG - NE)))
        br = jnp.pad(jnp.concatenate([b_grp[i], b_rt[i].reshape(NE)]), (0, 128 - NG - NE)).reshape(1, 128)

        mab, qbf, kbf, vbf, k_p, v_p, lf_p, c_p, ctail = _prompt_in(
            xp, gmix, win_bf, conv_w[i], lng, lnb, w_s[i], bse, bf_pad, gout)
        ct = jnp.swapaxes(c_p.reshape(NB, L, H_C), 1, 2)
        mc = _prompt_attn(qbf, kbf, vbf, ct, gout)
        xmid_p, h2_p, info_p, cnt_p = _out_route(mab, mc, xp, w_o[i].astype(BF16), gffn, wr.astype(BF16), br,
                                                 zero_cnt, mode="bf16", tm=TM_C)

        mab_s, q_s, k_s, v_s, lf_s128, z_s, vn_s = _sample_in(
            xs, gmix, win_pad, conv_w[i], state_conv[i, :, 0], state_conv[i, :, 1], lng, lnb, ws0, bs0,
            bf_pad, gout)
        lf_s = lf_s128[:, :H_C]
        q8 = q_s.reshape(NS, H_C, HD)
        mc_s = _sample_attn(pt_flat, jnp.broadcast_to(q8[..., None], (NS, H_C, HD, PAGE)), q8,
                            k_s.reshape(NS, H_C, HD), v_s.reshape(NS, H_C, HD, 1), lf_s.reshape(NS, H_C, 1),
                            gout[:, W_A + W_B:].reshape(H_C, HD, 1), ckt, cvt, clft, i)
        xmid_s, h2_s, info_s, cnt_all = _out_route(mab_s, mc_s.reshape(NS, W_C), xs, w_o[i], gffn, wr, br,
                                                   cnt_p, mode="x3", tm=NS)

        tile_e, tile_cnt, pos, zrow = _schedule(cnt_all, jnp.concatenate([info_p[:, :8], info_s[:, :8]], axis=0))
        xs_sorted = _dispatch(pos, zrow, h2_p, h2_s)
        y_sorted = _experts(tile_e, tile_cnt, xs_sorted, wg_all, wu_all, wd_all, i)

        xp, yp = _combine_ple(pos, xmid_p, info_p, p_prompt[i].reshape(T, D_PLE), w_ple[i].astype(BF16),
                              w_ple_gate[i].astype(BF16), gple, gfin, y_sorted, mode="bf16", tm=TM_F, slot0=0)
        xs, ys = _combine_ple(pos, xmid_s, info_s, p_sample[i].reshape(NS, D_PLE), w_ple[i],
                              w_ple_gate[i], gple, gfin, y_sorted, mode="x3", tm=NS, slot0=2 * T)

        outs["kp"].append(k_p.reshape(NB, L, H_C, HD))
        outs["vp"].append(v_p.reshape(NB, L, H_C, HD))
        outs["lfp"].append(lf_p.reshape(NB, L, H_C))
        outs["cvp"].append(ctail)
        outs["ks"].append(k_s.reshape(NS, 1, H_C, HD))
        outs["vs"].append(v_s.reshape(NS, 1, H_C, HD))
        outs["lfs"].append(lf_s.reshape(NS, 1, H_C))
        outs["cvs"].append(jnp.stack([state_conv[i, :, 1], z_s], axis=1))
        outs["chv"].append(vn_s.reshape(NS, 1, W_B))

    st = lambda k: jnp.stack(outs[k])
    return (yp.reshape(NB, L, D), ys.reshape(NS, 1, D), st("kp"), st("vp"), st("lfp"), st("cvp"),
            st("ks"), st("vs"), st("lfs"), st("cvs"), st("chv"))
```

```python
import functools

import jax
import jax.numpy as jnp
from jax import lax
from jax.experimental import pallas as pl
from jax.experimental.pallas import tpu as pltpu

F32 = jnp.float32
BF16 = jnp.bfloat16
I32 = jnp.int32

D = 1024
NB = 8
L = 2048
T = NB * L
DEPTH = 2
NS = 32
PAGE = 128
N_PAGES = 64
H_C = 8
HD = 64
W_A = 256
W_B = 256
W_C = 512
D_IN = 2824
D_IN_PAD = 2944
NG = 4
EPG = 8
NE = NG * EPG
DE = 512
D_PLE = 256
CHUNK = 128
EPS = 1e-6
NEG = -jnp.inf

NCH = D // 128
T_ALL = T + NS
S_SLOTS = 2 * T_ALL
TMX = 256
NT = S_SLOTS // TMX + NE
S_PAD = NT * TMX
N_SPARE = NT - (S_SLOTS + TMX - 1) // TMX

TM_A = 512
TQ = 512
TM_C = 512
TM_F = 512
PP = 32
NPG = N_PAGES // PP

VMEM_LIMIT = 56 * 1024 * 1024


def _cparams(sem):
    return pltpu.CompilerParams(dimension_semantics=sem, vmem_limit_bytes=VMEM_LIMIT)


def _split(a):
    hi = a.astype(BF16)
    lo = (a - hi.astype(F32)).astype(BF16)
    return hi, lo


def _dot(a, b):
    return jnp.dot(a, b, preferred_element_type=F32)


def _dot_nt(a, b):
    return lax.dot_general(a, b, (((1,), (1,)), ((), ())), preferred_element_type=F32)


def _mm(a, b, mode):
    if mode == "bf16":
        return _dot(a.astype(BF16), b.astype(BF16))
    a_hi, a_lo = _split(a)
    b_hi, b_lo = _split(b)
    return _dot(a_hi, b_hi) + _dot(a_lo, b_hi) + _dot(a_hi, b_lo)


def _rms(x, g):
    return x * lax.rsqrt(jnp.mean(x * x, axis=-1, keepdims=True) + EPS) * g


def _log_sigmoid(x):
    return jnp.minimum(x, 0.0) - jnp.log1p(jnp.exp(-jnp.abs(x)))


def _sigmoid(x):
    return 1.0 / (1.0 + jnp.exp(-x))


def _prompt_in_kernel(*refs, merge, precise):
    (x_ref, gmix_ref, win_ref, convw_ref, lng_ref, lnb_ref, ws_ref, bse_ref, bf_ref, gout_ref) = refs[:10]
    refs = refs[10:]
    if precise:
        wlo_ref, refs = refs[0], refs[1:]
    if merge:
        (kprev_ref, vprev_ref, mab_ref, qbf_ref, kbf_ref, vbf_ref, kall_ref, vall_ref, logf_ref, c_ref, ctail_ref,
         zbuf, ccar, kst, vst, ksem) = refs
    else:
        (mab_ref, qbf_ref, kbf_ref, vbf_ref, k_ref, v_ref, logf_ref, c_ref, ctail_ref, zbuf, ccar) = refs
    tm = TM_A
    i = pl.program_id(0)

    @pl.when(i % (L // tm) == 0)
    def _():
        zbuf[0:8, :] = jnp.zeros((8, W_A), F32)
        ccar[...] = jnp.zeros_like(ccar)

    h = _rms(x_ref[...], gmix_ref[...])
    hb = h.astype(BF16)

    def proj(lo, hi):
        return _dot(hb, win_ref[:, lo:hi])

    if precise:
        h_lo = (h - hb.astype(F32)).astype(BF16)

        def proj_kv(lo, hi):
            return proj(lo, hi) + _dot(h_lo, win_ref[:, lo:hi]) + _dot(hb, wlo_ref[:, lo:hi])
    else:
        proj_kv = proj

    pa = proj(0, 3 * W_A)
    z = pa[:, W_A:2 * W_A] * pa[:, 2 * W_A:3 * W_A]
    zbuf[8:8 + tm, :] = z
    z1 = zbuf[7:7 + tm, :]
    z2 = zbuf[6:6 + tm, :]
    cw = convw_ref[...]
    y_a = pa[:, 0:W_A] * (cw[0:1, :] * z2 + cw[1:2, :] * z1 + cw[2:3, :] * z)
    tail = z[tm - 2:tm, :]
    ctail_ref[...] = tail
    zbuf[6:8, :] = tail
    gout = gout_ref[...]
    mab_ref[:, 0:W_A] = _rms(y_a, gout[:, 0:W_A]).astype(BF16)

    o = 3 * W_A
    pb = proj(o, o + 2 * W_B)
    u_b = pb[:, 0:W_B]
    v_b = pb[:, W_B:2 * W_B]
    mu = jnp.mean(v_b, axis=-1, keepdims=True)
    vc = v_b - mu
    var = jnp.mean(vc * vc, axis=-1, keepdims=True)
    vn = vc * lax.rsqrt(var + EPS) * lng_ref[...] + lnb_ref[...]
    vnb = vn.astype(BF16)
    r_i = lax.broadcasted_iota(I32, (CHUNK, CHUNK), 0)
    c_i = lax.broadcasted_iota(I32, (CHUNK, CHUNK), 1)
    lane_head = lax.broadcasted_iota(I32, (CHUNK, W_B), 1) // HD
    ws_t = [jnp.where(r_i >= c_i, ws_ref[hh], 0.0).astype(BF16) for hh in range(4)]
    parts = []
    for cidx in range(tm // CHUNK):
        vchunk = vnb[cidx * CHUNK:(cidx + 1) * CHUNK, :]
        sc = jnp.zeros((CHUNK, W_B), F32)
        for hh in range(4):
            sc = jnp.where(lane_head == hh, _dot(ws_t[hh], vchunk), sc)
        parts.append(sc + bse_ref[...])
    s = jnp.concatenate(parts, axis=0)
    y_b = u_b * s
    mab_ref[:, W_A:W_A + W_B] = _rms(y_b, gout[:, W_A:W_A + W_B]).astype(BF16)

    o = 3 * W_A + 2 * W_B
    q = proj(o, o + W_C)
    k = proj_kv(o + W_C, o + 2 * W_C)
    v = proj_kv(o + 2 * W_C, o + 3 * W_C)
    f = proj_kv(o + 3 * W_C, D_IN_PAD)
    qbf_ref[...] = (q * (HD ** -0.5)).astype(BF16)
    if merge:
        n = pl.num_programs(0)
        slot = i % 2

        def copies(step, s):
            rows = pl.ds(pl.multiple_of(step * tm, tm), tm)
            return (pltpu.make_async_copy(kst.at[s], kall_ref.at[DEPTH - 1, rows], ksem.at[s, 0]),
                    pltpu.make_async_copy(vst.at[s], vall_ref.at[DEPTH - 1, rows], ksem.at[s, 1]),
                    pltpu.make_async_copy(kprev_ref.at[rows], kall_ref.at[0, rows], ksem.at[s, 2]),
                    pltpu.make_async_copy(vprev_ref.at[rows], vall_ref.at[0, rows], ksem.at[s, 3]))

        @pl.when(i >= 2)
        def _():
            for cp in copies(i - 2, slot):
                cp.wait()

        kst[slot] = k
        vst[slot] = v
        for cp in copies(i, slot):
            cp.start()

        @pl.when(i == n - 1)
        def _():
            for cp in copies(i - 1, 1 - slot) + copies(i, slot):
                cp.wait()
    else:
        k_ref[...] = k
        v_ref[...] = v
    kbf_ref[...] = k.astype(BF16)
    vbf_ref[...] = v.astype(BF16)
    lf = _log_sigmoid(f + bf_ref[...])
    logf_ref[...] = lf[:, 0:H_C]

    p1 = lf.astype(BF16)
    r1 = lf - p1.astype(F32)
    p2 = r1.astype(BF16)
    p3 = (r1 - p2.astype(F32)).astype(BF16)
    tr = lax.broadcasted_iota(I32, (tm, tm), 0) >= lax.broadcasted_iota(I32, (tm, tm), 1)
    trb = jnp.where(tr, 1.0, 0.0).astype(BF16)
    cs = _dot(trb, p1) + _dot(trb, p2) + _dot(trb, p3) + ccar[...]
    c_ref[...] = cs
    ccar[...] = cs[tm - 1:tm, :]


def _prompt_in(x, gmix, win_bf, convw, lng, lnb, ws, bse, bf_pad, gout, win_lo=None, kv_prev=None):
    tm = TM_A
    n = T // tm
    merge = kv_prev is not None
    precise = win_lo is not None
    full = lambda shape: pl.BlockSpec(shape, lambda i: (0,) * len(shape))
    tok = lambda w: pl.BlockSpec((tm, w), lambda i: (i, 0))
    hbm = pl.BlockSpec(memory_space=pl.ANY)
    kv_spec = hbm if merge else tok(W_C)
    kv_shape = jax.ShapeDtypeStruct((DEPTH, T, W_C) if merge else (T, W_C), F32)
    scratch = [pltpu.VMEM((tm + 8, W_A), F32), pltpu.VMEM((1, 128), F32)]
    if merge:
        scratch += [pltpu.VMEM((2, tm, W_C), F32), pltpu.VMEM((2, tm, W_C), F32), pltpu.SemaphoreType.DMA((2, 4))]
    return pl.pallas_call(
        functools.partial(_prompt_in_kernel, merge=merge, precise=precise),
        grid=(n,),
        in_specs=[tok(D), full((1, D)), full((D, D_IN_PAD)), full((3, W_A)), full((1, W_B)),
                  full((1, W_B)), full((4, CHUNK, CHUNK)), full((CHUNK, W_B)), full((1, 128)),
                  full((1, D))] + ([full((D, D_IN_PAD))] if precise else []) + ([hbm, hbm] if merge else []),
        out_specs=[tok(W_A + W_B), tok(W_C), tok(W_C), tok(W_C), kv_spec, kv_spec, tok(H_C), tok(128),
                   pl.BlockSpec((None, 2, W_A), lambda i: (i // (L // tm), 0, 0))],
        out_shape=[jax.ShapeDtypeStruct((T, W_A + W_B), BF16),
                   jax.ShapeDtypeStruct((T, W_C), BF16),
                   jax.ShapeDtypeStruct((T, W_C), BF16),
                   jax.ShapeDtypeStruct((T, W_C), BF16),
                   kv_shape, kv_shape,
                   jax.ShapeDtypeStruct((T, H_C), F32),
                   jax.ShapeDtypeStruct((T, 128), F32),
                   jax.ShapeDtypeStruct((NB, 2, W_A), F32)],
        scratch_shapes=scratch,
        compiler_params=_cparams(("arbitrary",)),
        name="prompt_in",
    )(x, gmix, win_bf, convw, lng, lnb, ws, bse, bf_pad, gout, *((win_lo,) if precise else ()), *(kv_prev or ()))


def _prompt_attn_kernel(q_ref, k_ref, v_ref, ct_ref, gout_ref, o_ref, yc):
    tq = TQ
    qi = pl.program_id(1)
    left = lax.broadcasted_iota(I32, (1, 128), 1) < HD
    row = lax.broadcasted_iota(I32, (tq, tq), 0)
    col = lax.broadcasted_iota(I32, (tq, tq), 1)
    zero = jnp.zeros((), BF16)

    for hp in range(H_C // 2):
        ls = slice(hp * 128, (hp + 1) * 128)
        q2 = q_ref[:, ls]
        q_a = jnp.where(left, q2, zero)
        q_b = jnp.where(left, zero, q2)
        h_a, h_b = 2 * hp, 2 * hp + 1

        def block(ki, carry, masked):
            m_a, l_a, m_b, l_b, acc = carry
            ks = pl.multiple_of(ki * tq, tq)
            k2 = k_ref[pl.ds(ks, tq), ls]
            v2 = v_ref[pl.ds(ks, tq), ls]

            def one(qh, h, m, l):
                s = _dot_nt(qh, k2) - ct_ref[h:h + 1, pl.ds(ks, tq)]
                if masked:
                    s = jnp.where(col <= row, s, NEG)
                mn = jnp.maximum(m, jnp.max(s, axis=-1, keepdims=True))
                a = jnp.exp(m - mn)
                p = jnp.exp(s - mn)
                ln = a * l + jnp.sum(p, axis=-1, keepdims=True)
                return mn, ln, a, _dot(p.astype(BF16), v2)

            m_a, l_a, a_a, pv_a = one(q_a, h_a, m_a, l_a)
            m_b, l_b, a_b, pv_b = one(q_b, h_b, m_b, l_b)
            acc = acc * jnp.where(left, a_a, a_b) + jnp.where(left, pv_a, pv_b)
            return m_a, l_a, m_b, l_b, acc

        init = (jnp.full((tq, 1), NEG, F32), jnp.zeros((tq, 1), F32),
                jnp.full((tq, 1), NEG, F32), jnp.zeros((tq, 1), F32),
                jnp.zeros((tq, 128), F32))
        carry = lax.fori_loop(0, qi, lambda ki, c: block(ki, c, False), init)
        m_a, l_a, m_b, l_b, acc = block(qi, carry, True)
        yc[:, ls] = acc / jnp.where(left, l_a, l_b)

    o_ref[...] = _rms(yc[...], gout_ref[:, W_A + W_B:]).astype(BF16)


def _prompt_attn(qbf, kbf, vbf, ct, gout):
    nq = L // TQ
    return pl.pallas_call(
        _prompt_attn_kernel,
        grid=(NB, nq),
        in_specs=[pl.BlockSpec((TQ, W_C), lambda n, qi: (n * nq + qi, 0)),
                  pl.BlockSpec((L, W_C), lambda n, qi: (n, 0)),
                  pl.BlockSpec((L, W_C), lambda n, qi: (n, 0)),
                  pl.BlockSpec((None, H_C, L), lambda n, qi: (n, 0, 0)),
                  pl.BlockSpec((1, D), lambda n, qi: (0, 0))],
        out_specs=pl.BlockSpec((TQ, W_C), lambda n, qi: (n * nq + qi, 0)),
        out_shape=jax.ShapeDtypeStruct((T, W_C), BF16),
        scratch_shapes=[pltpu.VMEM((TQ, W_C), F32)],
        compiler_params=_cparams(("arbitrary", "arbitrary")),
        name="prompt_attn",
    )(qbf, kbf, vbf, ct, gout)


TAIL = 8


def _mm3(a_hi, a_lo, b_hi, b_lo):
    return _dot(a_hi, b_hi) + _dot(a_lo, b_hi) + _dot(a_hi, b_lo)


def _tail_proj_kernel(x_ref, gmix_ref, win_ref, convw_ref, lng_ref, lnb_ref, ws_ref, bse_ref, gout_ref,
                      mab_ref, q_ref):
    h_hi, h_lo = _split(_rms(x_ref[...], gmix_ref[...]))

    def proj(lo, hi):
        w_hi, w_lo = _split(win_ref[:, lo:hi])
        return _mm3(h_hi, h_lo, w_hi, w_lo)

    pa = proj(0, 3 * W_A)
    pb = proj(3 * W_A, 3 * W_A + 2 * W_B)
    o = 3 * W_A + 2 * W_B
    q = proj(o, o + W_C) * (HD ** -0.5)
    cw = convw_ref[...]
    gout = gout_ref[...]
    lo8 = CHUNK - TAIL
    r_i = lax.broadcasted_iota(I32, (TAIL, CHUNK), 0) + lo8
    c_i = lax.broadcasted_iota(I32, (TAIL, CHUNK), 1)
    lane_head = lax.broadcasted_iota(I32, (TAIL, W_B), 1) // HD
    ws_t = [_split(jnp.where(c_i <= r_i, ws_ref[hh, lo8:CHUNK, :], 0.0)) for hh in range(4)]
    for n in range(NB):
        r0 = n * CHUNK
        t0 = r0 + lo8
        z = pa[r0:r0 + CHUNK, W_A:2 * W_A] * pa[r0:r0 + CHUNK, 2 * W_A:3 * W_A]
        y_a = pa[t0:t0 + TAIL, 0:W_A] * (cw[0:1, :] * z[lo8 - 2:CHUNK - 2] + cw[1:2, :] * z[lo8 - 1:CHUNK - 1]
                                         + cw[2:3, :] * z[lo8:CHUNK])
        v_b = pb[r0:r0 + CHUNK, W_B:2 * W_B]
        mu = jnp.mean(v_b, axis=-1, keepdims=True)
        vc = v_b - mu
        var = jnp.mean(vc * vc, axis=-1, keepdims=True)
        vn_hi, vn_lo = _split(vc * lax.rsqrt(var + EPS) * lng_ref[...] + lnb_ref[...])
        s = jnp.zeros((TAIL, W_B), F32)
        for hh in range(4):
            s = jnp.where(lane_head == hh, _mm3(ws_t[hh][0], ws_t[hh][1], vn_hi, vn_lo), s)
        y_b = pb[t0:t0 + TAIL, 0:W_B] * (s + bse_ref[lo8:CHUNK, :])
        rows = slice(n * TAIL, (n + 1) * TAIL)
        mab_ref[rows, 0:W_A] = _rms(y_a, gout[:, 0:W_A])
        mab_ref[rows, W_A:W_A + W_B] = _rms(y_b, gout[:, W_A:W_A + W_B])
        q_ref[rows, :] = q[t0:t0 + TAIL, :]


def _tail_proj(x_chunk, gmix, win, convw, lng, lnb, ws, bse, gout):
    return pl.pallas_call(
        _tail_proj_kernel,
        out_shape=[jax.ShapeDtypeStruct((NB * TAIL, W_A + W_B), F32), jax.ShapeDtypeStruct((NB * TAIL, W_C), F32)],
        compiler_params=pltpu.CompilerParams(vmem_limit_bytes=VMEM_LIMIT),
        name="tail_proj",
    )(x_chunk, gmix, win, convw, lng, lnb, ws, bse, gout)


def _tail_attn_kernel(q_ref, k_ref, v_ref, c_ref, gout_ref, mc_ref):
    q8 = q_ref[...]
    lane_head = lax.broadcasted_iota(I32, (TAIL, W_C), 1) // HD
    pieces = [jnp.where(lane_head == h, q8, 0.0) for h in range(H_C)]
    pieces.append(jnp.zeros((128 - H_C * TAIL, W_C), F32))
    q_hi, q_lo = _split(jnp.concatenate(pieces, axis=0))
    k_hi, k_lo = _split(k_ref[...])
    st = _dot_nt(k_hi, q_hi) + _dot_nt(k_lo, q_hi) + _dot_nt(k_hi, q_lo)
    c = c_ref[...]
    c1 = c.astype(BF16)
    r1 = c - c1.astype(F32)
    c2 = r1.astype(BF16)
    c3 = (r1 - c2.astype(F32)).astype(BF16)
    e_r = lax.broadcasted_iota(I32, (128, 128), 0)
    e_c = lax.broadcasted_iota(I32, (128, 128), 1)
    expand = jnp.where((e_c // TAIL == e_r) & (e_r < H_C), 1.0, 0.0).astype(BF16)
    c_keys = _dot(c1, expand) + _dot(c2, expand) + _dot(c3, expand)
    key = lax.broadcasted_iota(I32, (L, 128), 0)
    qpos = L - TAIL + lax.broadcasted_iota(I32, (L, 128), 1) % TAIL
    st = jnp.where(key <= qpos, st - c_keys, NEG)
    p = jnp.exp(st - jnp.max(st, axis=0, keepdims=True))
    pt = p.T
    denom = jnp.sum(pt, axis=1, keepdims=True)
    p_hi, p_lo = _split(pt)
    v_hi, v_lo = _split(v_ref[...])
    o = _mm3(p_hi, p_lo, v_hi, v_lo) / denom
    o8 = jnp.zeros((TAIL, W_C), F32)
    for h in range(H_C):
        o8 = jnp.where(lane_head == h, o[h * TAIL:(h + 1) * TAIL, :], o8)
    mc_ref[...] = _rms(o8, gout_ref[:, W_A + W_B:])


def _tail_attn(q_t, k, v, c128, gout):
    return pl.pallas_call(
        _tail_attn_kernel,
        grid=(NB,),
        in_specs=[pl.BlockSpec((TAIL, W_C), lambda n: (n, 0)),
                  pl.BlockSpec((L, W_C), lambda n: (n, 0)),
                  pl.BlockSpec((L, W_C), lambda n: (n, 0)),
                  pl.BlockSpec((L, 128), lambda n: (n, 0)),
                  pl.BlockSpec((1, D), lambda n: (0, 0))],
        out_specs=pl.BlockSpec((TAIL, W_C), lambda n: (n, 0)),
        out_shape=jax.ShapeDtypeStruct((NB * TAIL, W_C), F32),
        compiler_params=_cparams(("arbitrary",)),
        name="tail_attn",
    )(q_t, k, v, c128, gout)


def _tail_route_kernel(mab_ref, mc_ref, x_ref, wo_ref, gffn_ref, wr_ref, br_ref, lg_ref):
    upd = _mm(mab_ref[...], wo_ref[0:W_A + W_B, :], "x3") + _mm(mc_ref[...], wo_ref[W_A + W_B:, :], "x3")
    h2 = _rms(x_ref[...] + upd, gffn_ref[...])
    lg_ref[...] = _mm(h2, wr_ref[...], "x3") + br_ref[...]


def _tail_route(mab_t, mc_t, x_tail, wo, gffn, wr, br):
    return pl.pallas_call(
        _tail_route_kernel,
        out_shape=jax.ShapeDtypeStruct((NB * TAIL, 128), F32),
        compiler_params=pltpu.CompilerParams(vmem_limit_bytes=VMEM_LIMIT),
        name="tail_route",
    )(mab_t, mc_t, x_tail, wo, gffn, wr, br)


def _route(logits):
    lane = lax.broadcasted_iota(I32, logits.shape, 1).astype(F32)
    big = jnp.float32(1e9)
    gl = jnp.where(lane < NG, logits, NEG)
    gmax = jnp.max(gl, axis=-1, keepdims=True)
    p_g = 1.0 / jnp.sum(jnp.exp(gl - gmax), axis=-1, keepdims=True)
    g_idx = jnp.min(jnp.where(gl == gmax, lane, big), axis=-1, keepdims=True)
    lo = NG + EPG * g_idx
    el = jnp.where((lane >= lo) & (lane < lo + EPG), logits, NEG)
    m1 = jnp.max(el, axis=-1, keepdims=True)
    i1 = jnp.min(jnp.where(el == m1, lane, big), axis=-1, keepdims=True)
    zsum = jnp.sum(jnp.exp(el - m1), axis=-1, keepdims=True)
    el2 = jnp.where(lane == i1, NEG, el)
    m2 = jnp.max(el2, axis=-1, keepdims=True)
    i2 = jnp.min(jnp.where(el2 == m2, lane, big), axis=-1, keepdims=True)
    p1 = 1.0 / zsum
    p2 = jnp.exp(m2 - m1) / zsum
    den = p1 + p2
    w1 = p_g * p1 / den
    w2 = p_g * p2 / den
    e1 = i1 - NG
    e2 = i2 - NG
    info = jnp.where(lane == 0, e1,
                     jnp.where(lane == 1, e2,
                               jnp.where(lane == 2, w1, jnp.where(lane == 3, w2, 0.0))))
    return info, e1, e2


def _out_route_kernel(*refs, mode, tm, use_tail):
    (mab_ref, mc_ref, x_ref, wo_ref, gffn_ref, wr_ref, br_ref, cin_ref) = refs[:8]
    if use_tail:
        tail_ref, xmid_ref, h2_ref, info_ref, cnt_ref, carry, lg_s = refs[8:]
    else:
        xmid_ref, h2_ref, info_ref, cnt_ref, carry = refs[8:]

    @pl.when(pl.program_id(0) == 0)
    def _():
        carry[...] = cin_ref[...]

    upd = _mm(mab_ref[...], wo_ref[0:W_A + W_B, :], mode) + _mm(mc_ref[...], wo_ref[W_A + W_B:, :], mode)
    xm = x_ref[...] + upd
    xmid_ref[...] = xm
    h2 = _rms(xm, gffn_ref[...])
    for c in range(NCH):
        h2_ref[pl.ds(c, tm, stride=NCH), :] = h2[:, c * 128:(c + 1) * 128]
    logits = _mm(h2, wr_ref[...], mode) + br_ref[...]
    if use_tail:
        tiles_per_seq = L // tm
        lg_s[...] = logits

        @pl.when(pl.program_id(0) % tiles_per_seq == tiles_per_seq - 1)
        def _():
            lg_s[tm - TAIL:tm, :] = tail_ref[...]
        logits = lg_s[...]
    info, e1, e2 = _route(logits)

    lane = lax.broadcasted_iota(I32, (tm, 128), 1).astype(F32)
    hit1 = lane == e1
    hit2 = lane == e2
    both = jnp.where(hit1, 1.0, 0.0) + jnp.where(hit2, 1.0, 0.0)
    earlier = lax.broadcasted_iota(I32, (tm, tm), 0) > lax.broadcasted_iota(I32, (tm, tm), 1)
    before = _dot(jnp.where(earlier, 1.0, 0.0).astype(BF16), both.astype(BF16)) + carry[...]
    r1 = jnp.sum(jnp.where(hit1, before, 0.0), axis=-1, keepdims=True)
    r2 = jnp.sum(jnp.where(hit2, before, 0.0), axis=-1, keepdims=True)
    info_ref[...] = jnp.where(lane == 4, r1, jnp.where(lane == 5, r2, info))
    total = carry[...] + jnp.sum(both, axis=0, keepdims=True)
    carry[...] = total
    cnt_ref[...] = total


def _out_route(mab, mc, x, wo, gffn, wr, br, cnt_in, tail=None, *, mode, tm):
    n_tok = x.shape[0]
    use_tail = tail is not None
    full = lambda shape: pl.BlockSpec(shape, lambda i: (0,) * len(shape))
    tok = lambda w: pl.BlockSpec((tm, w), lambda i: (i, 0))
    in_specs = [tok(W_A + W_B), tok(W_C), tok(D), full((D, D)), full((1, D)), full((D, 128)),
                full((1, 128)), full((1, 128))]
    scratch = [pltpu.VMEM((1, 128), F32)]
    args = [mab, mc, x, wo, gffn, wr, br, cnt_in]
    if use_tail:
        in_specs.append(pl.BlockSpec((None, TAIL, 128), lambda i: (i // (L // tm), 0, 0)))
        scratch.append(pltpu.VMEM((tm, 128), F32))
        args.append(tail)
    return pl.pallas_call(
        functools.partial(_out_route_kernel, mode=mode, tm=tm, use_tail=use_tail),
        grid=(n_tok // tm,),
        in_specs=in_specs,
        out_specs=[tok(D), pl.BlockSpec((tm * NCH, 128), lambda i: (i, 0)), tok(128), full((1, 128))],
        out_shape=[jax.ShapeDtypeStruct((n_tok, D), F32),
                   jax.ShapeDtypeStruct((n_tok * NCH, 128), F32),
                   jax.ShapeDtypeStruct((n_tok, 128), F32),
                   jax.ShapeDtypeStruct((1, 128), F32)],
        scratch_shapes=scratch,
        compiler_params=_cparams(("arbitrary",)),
        name="out_route_" + mode,
    )(*args)


TM_D = 512
DISPATCH_UNROLL = 4


def _dispatch_kernel(pos_ref, zrow_ref, h2p_ref, h2s_ref, xs_ref, zbuf, sem, zsem):
    i = pl.program_id(0)
    last = i == pl.num_programs(0) - 1

    @pl.when(i == 0)
    def _():
        zbuf[...] = jnp.zeros_like(zbuf)

        def zero_copy(e):
            return pltpu.make_async_copy(zbuf, xs_ref.at[pl.ds(zrow_ref[e], TMX)], zsem)
        for e in range(NE + N_SPARE):
            pl.when(zrow_ref[e] >= 0)(lambda e=e: zero_copy(e).start())
        for e in range(NE + N_SPARE):
            pl.when(zrow_ref[e] >= 0)(lambda e=e: zero_copy(e).wait())

    def copy_token(src_ref, r, slot):
        for k in range(2):
            pltpu.make_async_copy(src_ref.at[r], xs_ref.at[pos_ref[slot + k]], sem).start()

    def wait_rows(n):
        pltpu.make_async_copy(xs_ref.at[pl.ds(0, n)], xs_ref.at[pl.ds(0, n)], sem).wait()

    def body(j, c):
        for u in range(DISPATCH_UNROLL):
            r = j * DISPATCH_UNROLL + u
            copy_token(h2p_ref, r, 2 * (i * TM_D + r))
        return c
    lax.fori_loop(0, TM_D // DISPATCH_UNROLL, body, 0)

    @pl.when(last)
    def _():
        def sample_body(j, c):
            copy_token(h2s_ref, j, 2 * (T + j))
            return c
        lax.fori_loop(0, NS, sample_body, 0)
        wait_rows(2 * NS)

    wait_rows(2 * TM_D)


def _dispatch(pos, zrow, h2p, h2s):
    grid_spec = pltpu.PrefetchScalarGridSpec(
        num_scalar_prefetch=2,
        grid=(T // TM_D,),
        in_specs=[pl.BlockSpec((TM_D, NCH, 128), lambda i, pos, zrow: (i, 0, 0)),
                  pl.BlockSpec((NS, NCH, 128), lambda i, pos, zrow: (0, 0, 0))],
        out_specs=pl.BlockSpec(memory_space=pl.ANY),
        scratch_shapes=[pltpu.VMEM((TMX, NCH, 128), F32), pltpu.SemaphoreType.DMA(()),
                        pltpu.SemaphoreType.DMA(())],
    )
    return pl.pallas_call(
        _dispatch_kernel,
        grid_spec=grid_spec,
        out_shape=jax.ShapeDtypeStruct((S_PAD, NCH, 128), F32),
        compiler_params=_cparams(("arbitrary",)),
        name="dispatch",
    )(pos, zrow, h2p.reshape(T, NCH, 128), h2s.reshape(NS, NCH, 128))


def _expert_kernel(te_ref, tc_ref, tn_ref, tp_ref, xs_ref, wg_hbm, wu_hbm, wd_hbm, y_ref,
                   wg_f, wu_f, wd_f, wg_b, wu_b, wd_b, sem, *, layer):
    t = pl.program_id(0)
    slot = tp_ref[t]

    def fetch(e, s):
        return (pltpu.make_async_copy(wg_hbm.at[layer * NE + e], wg_f.at[s], sem.at[s, 0]),
                pltpu.make_async_copy(wu_hbm.at[layer * NE + e], wu_f.at[s], sem.at[s, 1]),
                pltpu.make_async_copy(wd_hbm.at[layer * NE + e], wd_f.at[s], sem.at[s, 2]))

    @pl.when(t == 0)
    def _():
        for cp in fetch(te_ref[0], 0):
            cp.start()

    @pl.when((t == 0) | (te_ref[t] != te_ref[jnp.maximum(t - 1, 0)]))
    def _():
        for cp in fetch(te_ref[t], slot):
            cp.wait()

        @pl.when(tn_ref[t] >= 0)
        def _():
            for cp in fetch(tn_ref[t], 1 - slot):
                cp.start()

        wg_b[...] = wg_f[slot].astype(BF16)
        wu_b[...] = wu_f[slot].astype(BF16)
        wd_b[...] = wd_f[slot].astype(BF16)

    @pl.when(tc_ref[t] > 0)
    def _():
        xb = jnp.concatenate([xs_ref[pl.ds(c, TMX, stride=NCH), :] for c in range(NCH)], axis=1).astype(BF16)
        a = _dot(xb, wg_b[...])
        b = _dot(xb, wu_b[...])
        act = (a * _sigmoid(a) * b).astype(BF16)
        y = _dot(act, wd_b[...])
        for c in range(NCH):
            y_ref[pl.ds(c, TMX, stride=NCH), :] = y[:, c * 128:(c + 1) * 128]

    @pl.when(tc_ref[t] == 0)
    def _():
        y_ref[...] = jnp.zeros_like(y_ref)


def _experts(tile_e, tile_cnt, tile_next, tile_par, xs_sorted, wg, wu, wd, layer):
    grid_spec = pltpu.PrefetchScalarGridSpec(
        num_scalar_prefetch=4,
        grid=(NT,),
        in_specs=[pl.BlockSpec((TMX * NCH, 128), lambda t, *_: (t, 0)),
                  pl.BlockSpec(memory_space=pl.ANY), pl.BlockSpec(memory_space=pl.ANY),
                  pl.BlockSpec(memory_space=pl.ANY)],
        out_specs=pl.BlockSpec((TMX * NCH, 128), lambda t, *_: (t, 0)),
        scratch_shapes=[pltpu.VMEM((2, D, DE), F32), pltpu.VMEM((2, D, DE), F32), pltpu.VMEM((2, DE, D), F32),
                        pltpu.VMEM((D, DE), BF16), pltpu.VMEM((D, DE), BF16), pltpu.VMEM((DE, D), BF16),
                        pltpu.SemaphoreType.DMA((2, 3))],
    )
    y = pl.pallas_call(
        functools.partial(_expert_kernel, layer=layer),
        grid_spec=grid_spec,
        out_shape=jax.ShapeDtypeStruct((S_PAD * NCH, 128), F32),
        compiler_params=_cparams(("arbitrary",)),
        name="experts",
    )(tile_e, tile_cnt, tile_next, tile_par, xs_sorted.reshape(S_PAD * NCH, 128), wg, wu, wd)
    return y.reshape(S_PAD, NCH, 128)


def _combine_ple_kernel(pos_ref, xmid_ref, info_ref, p_ref, wple_ref, wpg_ref, gple_ref, gfin_ref, ys_ref,
                        x_ref, y_ref, gbuf, x2s, sem, *, mode, tm, slot0):
    i = pl.program_id(0)
    n = pl.num_programs(0)

    def start_row(tile, buf, j):
        s = slot0 + 2 * (tile * tm + j)
        for k in range(2):
            dst = gbuf.at[buf, pl.ds(pl.multiple_of((k * tm + j) * NCH, NCH), NCH), :]
            pltpu.make_async_copy(ys_ref.at[pos_ref[s + k]], dst, sem.at[buf]).start()

    @pl.when(i == 0)
    def _():
        def body(j, c):
            start_row(0, 0, j)
            return c
        lax.fori_loop(0, tm, body, 0)

    buf = i % 2

    def compute(prefetch):
        nxt = 0
        def issue_some(count):
            nonlocal nxt
            if prefetch:
                for j in range(nxt, min(nxt + count, tm)):
                    start_row(i + 1, 1 - buf, j)
                nxt = min(nxt + count, tm)

        pltpu.make_async_copy(gbuf.at[buf], gbuf.at[buf], sem.at[buf]).wait()
        info = info_ref[...]
        w1 = info[:, 2:3]
        w2 = info[:, 3:4]
        for c in range(8):
            cs = slice(c * 128, (c + 1) * 128)
            y1 = gbuf[buf, pl.ds(c, tm, stride=NCH), :]
            y2 = gbuf[buf, pl.ds(tm * NCH + c, tm, stride=NCH), :]
            x2s[:, cs] = xmid_ref[:, cs] + w1 * y1 + w2 * y2
            issue_some(tm // 16)
        x2 = x2s[...]
        gate = _sigmoid(_mm(_rms(x2, gple_ref[...]), wpg_ref[...], mode))
        issue_some(tm // 4)
        x3 = x2 + _mm(p_ref[...], wple_ref[...], mode) * gate
        issue_some(tm)
        x_ref[...] = x3
        y_ref[...] = _rms(x3, gfin_ref[...])

    @pl.when(i + 1 < n)
    def _():
        compute(True)

    @pl.when(i + 1 == n)
    def _():
        compute(False)


def _combine_ple(pos, xmid, info, p, wple, wpg, gple, gfin, y_sorted, *, mode, tm, slot0):
    n_tok = xmid.shape[0]
    full = lambda shape: pl.BlockSpec(shape, lambda i, pos: (0,) * len(shape))
    tok = lambda w: pl.BlockSpec((tm, w), lambda i, pos: (i, 0))
    grid_spec = pltpu.PrefetchScalarGridSpec(
        num_scalar_prefetch=1,
        grid=(n_tok // tm,),
        in_specs=[tok(D), tok(128), tok(D_PLE), full((D_PLE, D)), full((D, D)), full((1, D)), full((1, D)),
                  pl.BlockSpec(memory_space=pl.ANY)],
        out_specs=[tok(D), tok(D)],
        scratch_shapes=[pltpu.VMEM((2, 2 * tm * NCH, 128), F32), pltpu.VMEM((tm, D), F32),
                        pltpu.SemaphoreType.DMA((2,))],
    )
    return pl.pallas_call(
        functools.partial(_combine_ple_kernel, mode=mode, tm=tm, slot0=slot0),
        grid_spec=grid_spec,
        out_shape=[jax.ShapeDtypeStruct((n_tok, D), F32), jax.ShapeDtypeStruct((n_tok, D), F32)],
        compiler_params=_cparams(("arbitrary",)),
        name="combine_ple_" + mode,
    )(pos, xmid, info, p, wple, wpg, gple, gfin, y_sorted)


def _sample_in_kernel(x_ref, gmix_ref, win_ref, convw_ref, s0_ref, s1_ref, lng_ref, lnb_ref,
                      ws0_ref, bs0_ref, bf_ref, gout_ref,
                      mab_ref, q_ref, k_ref, v_ref, logf_ref, z_ref, vn_ref):
    h = _rms(x_ref[...], gmix_ref[...])
    h_hi, h_lo = _split(h)

    def proj(lo, hi):
        w_hi, w_lo = _split(win_ref[:, lo:hi])
        return _dot(h_hi, w_hi) + _dot(h_lo, w_hi) + _dot(h_hi, w_lo)

    pa = proj(0, 3 * W_A)
    z = pa[:, W_A:2 * W_A] * pa[:, 2 * W_A:3 * W_A]
    cw = convw_ref[...]
    y_a = pa[:, 0:W_A] * (cw[0:1, :] * s0_ref[...] + cw[1:2, :] * s1_ref[...] + cw[2:3, :] * z)
    z_ref[...] = z
    gout = gout_ref[...]
    mab_ref[:, 0:W_A] = _rms(y_a, gout[:, 0:W_A])

    o = 3 * W_A
    pb = proj(o, o + 2 * W_B)
    v_b = pb[:, W_B:2 * W_B]
    mu = jnp.mean(v_b, axis=-1, keepdims=True)
    vc = v_b - mu
    var = jnp.mean(vc * vc, axis=-1, keepdims=True)
    vn = vc * lax.rsqrt(var + EPS) * lng_ref[...] + lnb_ref[...]
    vn_ref[...] = vn
    y_b = pb[:, 0:W_B] * (ws0_ref[...] * vn + bs0_ref[...])
    mab_ref[:, W_A:W_A + W_B] = _rms(y_b, gout[:, W_A:W_A + W_B])

    o = 3 * W_A + 2 * W_B
    q_ref[...] = proj(o, o + W_C) * (HD ** -0.5)
    k_ref[...] = proj(o + W_C, o + 2 * W_C)
    v_ref[...] = proj(o + 2 * W_C, o + 3 * W_C)
    f = proj(o + 3 * W_C, D_IN_PAD)
    logf_ref[...] = _log_sigmoid(f + bf_ref[...])


def _sample_in(x, gmix, win_pad, convw, s0, s1, lng, lnb, ws0, bs0, bf_pad, gout):
    shapes = [(NS, W_A + W_B), (NS, W_C), (NS, W_C), (NS, W_C), (NS, 128), (NS, W_A), (NS, W_B)]
    return pl.pallas_call(
        _sample_in_kernel,
        out_shape=[jax.ShapeDtypeStruct(s, F32) for s in shapes],
        compiler_params=pltpu.CompilerParams(vmem_limit_bytes=VMEM_LIMIT),
        name="sample_in",
    )(x, gmix, win_pad, convw, s0, s1, lng, lnb, ws0, bs0, bf_pad, gout)


def _sample_attn_kernel(pt_ref, qb_ref, q8_ref, kn_ref, vnt_ref, lfn_ref, gct_ref, *rest):
    k_refs = rest[0:PP]
    v_refs = rest[PP:2 * PP]
    lf_refs = rest[2 * PP:3 * PP]
    o_ref = rest[3 * PP]
    m_s, l_s, acc_s, tot_s = rest[3 * PP + 1:]
    j = pl.program_id(1)

    @pl.when(j == 0)
    def _():
        m_s[...] = jnp.full_like(m_s, NEG)
        l_s[...] = jnp.zeros_like(l_s)
        acc_s[...] = jnp.zeros_like(acc_s)
        tot_s[...] = jnp.zeros_like(tot_s)

    cn = lfn_ref[...]
    lane = lax.broadcasted_iota(I32, (H_C, PAGE), 1)

    tot = tot_s[...]
    scores = []
    for pp in reversed(range(PP)):
        lf = lf_refs[pp][...]
        inc = lf
        sh = 1
        while sh < PAGE:
            inc = inc + jnp.where(lane >= sh, pltpu.roll(inc, sh, axis=1), 0.0)
            sh *= 2
        page_tot = inc[:, PAGE - 1:PAGE]
        bias = (page_tot - inc) + tot + cn
        tot = tot + page_tot
        rows = [jnp.sum(k_refs[pp][h] * qb_ref[h], axis=0, keepdims=True) for h in range(H_C)]
        scores.append((pp, jnp.concatenate(rows, axis=0) + bias))
    tot_s[...] = tot

    m_old = m_s[...]
    m = m_old
    for _, s in scores:
        m = jnp.maximum(m, jnp.max(s, axis=-1, keepdims=True))
    a = jnp.exp(m_old - m)
    l = a * l_s[...]
    probs = []
    for pp, s in scores:
        p = jnp.exp(s - m)
        l = l + jnp.sum(p, axis=-1, keepdims=True)
        probs.append((pp, p))
    m_s[...] = m
    l_s[...] = l
    for h in range(H_C):
        acc_h = acc_s[h] * a[h:h + 1, :]
        for pp, p in probs:
            acc_h = acc_h + v_refs[pp][h] * p[h:h + 1, :]
        acc_s[h] = acc_h

    @pl.when(j == NPG - 1)
    def _():
        s_new = jnp.sum(q8_ref[...] * kn_ref[...], axis=-1, keepdims=True)
        mn = jnp.maximum(m, s_new)
        a2 = jnp.exp(m - mn)
        pn = jnp.exp(s_new - mn)
        lt = a2 * l + pn
        outs = []
        ms = jnp.zeros((1, 1), F32)
        for h in range(H_C):
            hs = slice(h, h + 1)
            o_h = (a2[hs, :] * jnp.sum(acc_s[h], axis=1, keepdims=True) + pn[hs, :] * vnt_ref[h]) / lt[hs, :]
            ms = ms + jnp.sum(o_h * o_h, axis=0, keepdims=True)
            outs.append(o_h)
        scale = lax.rsqrt(ms / W_C + EPS)
        for h in range(H_C):
            o_ref[h] = outs[h] * scale * gct_ref[h]


def _sample_attn(pt_flat, qb, q8, kn8, vnt, lfn3, gct, ckt, cvt, clft, layer):
    def page_map(nd):
        return lambda pp: (lambda n, j, pt: (layer, pt[n * N_PAGES + (NPG - 1 - j) * PP + pp]) + (0,) * nd)

    head = pl.BlockSpec((None, H_C, HD), lambda n, j, pt: (n, 0, 0))
    col = pl.BlockSpec((None, H_C, HD, 1), lambda n, j, pt: (n, 0, 0, 0))
    in_specs = [pl.BlockSpec((None, H_C, HD, PAGE), lambda n, j, pt: (n, 0, 0, 0)),
                head, head, col,
                pl.BlockSpec((None, H_C, 1), lambda n, j, pt: (n, 0, 0)),
                pl.BlockSpec((H_C, HD, 1), lambda n, j, pt: (0, 0, 0))]
    in_specs += [pl.BlockSpec((None, None, H_C, HD, PAGE), page_map(3)(pp)) for pp in range(PP)]
    in_specs += [pl.BlockSpec((None, None, H_C, HD, PAGE), page_map(3)(pp)) for pp in range(PP)]
    in_specs += [pl.BlockSpec((None, None, H_C, PAGE), page_map(2)(pp)) for pp in range(PP)]
    grid_spec = pltpu.PrefetchScalarGridSpec(
        num_scalar_prefetch=1,
        grid=(NS, NPG),
        in_specs=in_specs,
        out_specs=col,
        scratch_shapes=[pltpu.VMEM((H_C, 1), F32), pltpu.VMEM((H_C, 1), F32),
                        pltpu.VMEM((H_C, HD, PAGE), F32), pltpu.VMEM((H_C, 1), F32)],
    )
    return pl.pallas_call(
        _sample_attn_kernel,
        grid_spec=grid_spec,
        out_shape=jax.ShapeDtypeStruct((NS, H_C, HD, 1), F32),
        compiler_params=_cparams(("arbitrary", "arbitrary")),
        name="sample_attn",
    )(pt_flat, qb, q8, kn8, vnt, lfn3, gct, *([ckt] * PP), *([cvt] * PP), *([clft] * PP))


def _schedule(cnt_f, info_all):
    cnt = cnt_f[0, :NE].astype(I32)
    ntile = (cnt + TMX - 1) // TMX
    tcum = jnp.cumsum(ntile).astype(I32)
    tbase = tcum - ntile
    t = jnp.arange(NT, dtype=I32)
    ids = jnp.arange(NE, dtype=I32)
    last_e = jnp.max(jnp.where(ntile > 0, ids, 0))
    tile_e = jnp.minimum(jnp.sum((tcum[None, :] <= t[:, None]).astype(I32), axis=1), last_e)
    later = (ids[None, :] > ids[:, None]) & (ntile[None, :] > 0)
    next_e = jnp.min(jnp.where(later, ids[None, :], NE), axis=1)
    next_e = jnp.where(next_e < NE, next_e, -1)
    run = jnp.cumsum((ntile > 0).astype(I32)) - 1
    sel = tile_e[:, None] == ids[None, :]
    tile_next = jnp.sum(jnp.where(sel, next_e[None, :], 0), axis=1).astype(I32)
    tile_par = (jnp.sum(jnp.where(sel, run[None, :], 0), axis=1) % 2).astype(I32)
    within = t - jnp.sum(jnp.where(sel, tbase[None, :], 0), axis=1)
    tile_cnt = jnp.clip(jnp.sum(jnp.where(sel, cnt[None, :], 0), axis=1) - within * TMX, 0, TMX).astype(I32)
    eid = info_all[:, 0:2].astype(I32)
    rank = info_all[:, 4:6].astype(I32)
    base = jnp.sum(jnp.where(eid[:, :, None] == jnp.arange(NE, dtype=I32)[None, None, :],
                             (tbase * TMX)[None, None, :], 0), axis=-1)
    pos = (base + rank).reshape(-1).astype(I32)
    last_tile = jnp.where(ntile > 0, tcum - 1, -1)
    spare = tcum[NE - 1] + jnp.arange(N_SPARE, dtype=I32)
    ztile = jnp.concatenate([last_tile, jnp.where(spare < NT, spare, -1)])
    zrow = jnp.where(ztile >= 0, ztile * TMX, -1).astype(I32)
    return (tile_e, tile_cnt, tile_next, tile_par), pos, zrow


def kernel(x_prompt, x_sample, p_prompt, p_sample, cache_k, cache_v, cache_logf, state_conv, page_table, g_mix, w_in, conv_w, ln_g, ln_b, w_s, b_s, b_f, g_out, w_o, g_ffn, w_grp, b_grp, w_rt, b_rt, w_gate, w_up, w_down, g_ple, w_ple, w_ple_gate, g_final):
    n_pool = cache_k.shape[1]
    xp = x_prompt.reshape(T, D)
    xs = x_sample.reshape(NS, D)
    pt_flat = page_table.reshape(-1).astype(I32)
    del n_pool
    ckt = jnp.transpose(cache_k, (0, 1, 3, 4, 2))
    cvt = jnp.transpose(cache_v, (0, 1, 3, 4, 2))
    clft = jnp.swapaxes(cache_logf, 2, 3)
    zero_cnt = jnp.zeros((1, 128), F32)
    wg_all = w_gate.reshape(DEPTH * NE, D, DE)
    wu_all = w_up.reshape(DEPTH * NE, D, DE)
    wd_all = w_down.reshape(DEPTH * NE, DE, D)
    gfin = g_final.reshape(1, D)

    assert DEPTH == 2
    outs = {k: [] for k in ("lfp", "cvp", "ks", "vs", "lfs", "cvs", "chv")}
    yp = ys = k_p = v_p = None
    for i in range(DEPTH):
        gmix = g_mix[i].reshape(1, D)
        gout = g_out[i].reshape(1, D)
        gffn = g_ffn[i].reshape(1, D)
        gple = g_ple[i].reshape(1, D)
        win_pad = jnp.pad(w_in[i], ((0, 0), (0, D_IN_PAD - D_IN)))
        win_bf = win_pad.astype(BF16)
        bf_pad = jnp.pad(b_f[i], (0, 128 - H_C)).reshape(1, 128)
        lng = ln_g[i].reshape(1, W_B)
        lnb = ln_b[i].reshape(1, W_B)
        bse = jnp.repeat(b_s[i].T, HD, axis=1)
        ws0 = jnp.repeat(w_s[i][:, 0, 0], HD).reshape(1, W_B)
        bs0 = jnp.repeat(b_s[i][:, 0], HD).reshape(1, W_B)
        wr = jnp.pad(jnp.concatenate([w_grp[i], w_rt[i].reshape(D, NE)], axis=1), ((0, 0), (0, 128 - NG - NE)))
        br = jnp.pad(jnp.concatenate([b_grp[i], b_rt[i].reshape(NE)]), (0, 128 - NG - NE)).reshape(1, 128)

        first = i == 0
        win_lo = (win_pad - win_bf.astype(F32)).astype(BF16) if first else None
        mab, qbf, kbf, vbf, k_p, v_p, lf_p, c_p, ctail = _prompt_in(
            xp, gmix, win_bf, conv_w[i], lng, lnb, w_s[i], bse, bf_pad, gout, win_lo=win_lo,
            kv_prev=(k_p, v_p) if i == DEPTH - 1 else None)
        ct = jnp.swapaxes(c_p[:, :H_C].reshape(NB, L, H_C), 1, 2)
        mc = _prompt_attn(qbf, kbf, vbf, ct, gout)
        tail = None
        if first:
            x_seq = xp.reshape(NB, L, D)
            mab_t, q_t = _tail_proj(x_seq[:, L - CHUNK:].reshape(NB * CHUNK, D), gmix,
                                    win_pad[:, :3 * W_A + 2 * W_B + W_C], conv_w[i], lng, lnb, w_s[i], bse, gout)
            mc_t = _tail_attn(q_t, k_p, v_p, c_p, gout)
            tail = _tail_route(mab_t, mc_t, x_seq[:, L - TAIL:].reshape(NB * TAIL, D), w_o[i], gffn, wr,
                               br).reshape(NB, TAIL, 128)
        xmid_p, h2_p, info_p, cnt_p = _out_route(mab, mc, xp, w_o[i].astype(BF16), gffn, wr.astype(BF16), br,
                                                 zero_cnt, tail, mode="bf16", tm=TM_C)

        mab_s, q_s, k_s, v_s, lf_s128, z_s, vn_s = _sample_in(
            xs, gmix, win_pad, conv_w[i], state_conv[i, :, 0], state_conv[i, :, 1], lng, lnb, ws0, bs0,
            bf_pad, gout)
        lf_s = lf_s128[:, :H_C]
        q8 = q_s.reshape(NS, H_C, HD)
        mc_s = _sample_attn(pt_flat, jnp.broadcast_to(q8[..., None], (NS, H_C, HD, PAGE)), q8,
                            k_s.reshape(NS, H_C, HD), v_s.reshape(NS, H_C, HD, 1), lf_s.reshape(NS, H_C, 1),
                            gout[:, W_A + W_B:].reshape(H_C, HD, 1), ckt, cvt, clft, i)
        xmid_s, h2_s, info_s, cnt_all = _out_route(mab_s, mc_s.reshape(NS, W_C), xs, w_o[i], gffn, wr, br,
                                                   cnt_p, mode="x3", tm=NS)

        tiles, pos, zrow = _schedule(cnt_all, jnp.concatenate([info_p[:, :8], info_s[:, :8]], axis=0))
        xs_sorted = _dispatch(pos, zrow, h2_p, h2_s)
        y_sorted = _experts(*tiles, xs_sorted, wg_all, wu_all, wd_all, i)

        xp, yp = _combine_ple(pos, xmid_p, info_p, p_prompt[i].reshape(T, D_PLE), w_ple[i].astype(BF16),
                              w_ple_gate[i].astype(BF16), gple, gfin, y_sorted, mode="bf16", tm=TM_F, slot0=0)
        xs, ys = _combine_ple(pos, xmid_s, info_s, p_sample[i].reshape(NS, D_PLE), w_ple[i],
                              w_ple_gate[i], gple, gfin, y_sorted, mode="x3", tm=NS, slot0=2 * T)

        outs["lfp"].append(lf_p.reshape(NB, L, H_C))
        outs["cvp"].append(ctail)
        outs["ks"].append(k_s.reshape(NS, 1, H_C, HD))
        outs["vs"].append(v_s.reshape(NS, 1, H_C, HD))
        outs["lfs"].append(lf_s.reshape(NS, 1, H_C))
        outs["cvs"].append(jnp.stack([state_conv[i, :, 1], z_s], axis=1))
        outs["chv"].append(vn_s.reshape(NS, 1, W_B))

    st = lambda k: jnp.stack(outs[k])
    return (yp.reshape(NB, L, D), ys.reshape(NS, 1, D), k_p.reshape(DEPTH, NB, L, H_C, HD),
            v_p.reshape(DEPTH, NB, L, H_C, HD), st("lfp"), st("cvp"),
            st("ks"), st("vs"), st("lfs"), st("cvs"), st("chv"))
```

```python
import functools

import jax
import jax.numpy as jnp
from jax import lax
from jax.experimental import pallas as pl
from jax.experimental.pallas import tpu as pltpu

F32 = jnp.float32
BF16 = jnp.bfloat16
I32 = jnp.int32

D = 1024
NB = 8
L = 2048
T = NB * L
DEPTH = 2
NS = 32
PAGE = 128
N_PAGES = 64
H_C = 8
HD = 64
W_A = 256
W_B = 256
W_C = 512
D_IN = 2824
D_IN_PAD = 2944
KV_COL0 = 3 * W_A + 2 * W_B + W_C
NG = 4
EPG = 8
NE = NG * EPG
DE = 512
D_PLE = 256
CHUNK = 128
EPS = 1e-6
NEG = -jnp.inf

NCH = D // 128
T_ALL = T + NS
S_SLOTS = 2 * T_ALL
TMX = 256
NT = S_SLOTS // TMX + NE
S_PAD = NT * TMX
N_SPARE = NT - (S_SLOTS + TMX - 1) // TMX

TM_A = 512
TQ = 512
TM_C = 512
TM_F = 512
PP = 32
NPG = N_PAGES // PP

VMEM_LIMIT = 56 * 1024 * 1024


def _cparams(sem):
    return pltpu.CompilerParams(dimension_semantics=sem, vmem_limit_bytes=VMEM_LIMIT)


def _split(a):
    hi = a.astype(BF16)
    lo = (a - hi.astype(F32)).astype(BF16)
    return hi, lo


def _dot(a, b):
    return jnp.dot(a, b, preferred_element_type=F32)


def _dot_nt(a, b):
    return lax.dot_general(a, b, (((1,), (1,)), ((), ())), preferred_element_type=F32)


def _mm(a, b, mode):
    if mode == "bf16":
        return _dot(a.astype(BF16), b.astype(BF16))
    a_hi, a_lo = _split(a)
    b_hi, b_lo = _split(b)
    return _dot(a_hi, b_hi) + _dot(a_lo, b_hi) + _dot(a_hi, b_lo)


def _rms(x, g):
    return x * lax.rsqrt(jnp.mean(x * x, axis=-1, keepdims=True) + EPS) * g


def _log_sigmoid(x):
    return jnp.minimum(x, 0.0) - jnp.log1p(jnp.exp(-jnp.abs(x)))


def _sigmoid(x):
    return 1.0 / (1.0 + jnp.exp(-x))


def _prompt_in_kernel(*refs, precise):
    (x_ref, gmix_ref, win_ref, convw_ref, lng_ref, lnb_ref, ws_ref, bse_ref, bf_ref, gout_ref) = refs[:10]
    refs = refs[10:]
    if precise:
        wkv_ref, refs = refs[0], refs[1:]
        wlo_s, refs = refs[-1], refs[:-1]
    (mab_ref, qbf_ref, kbf_ref, vbf_ref, k_ref, v_ref, logf_ref, c_ref, ctail_ref, zbuf, ccar) = refs
    tm = TM_A
    i = pl.program_id(0)

    if precise:
        @pl.when(i == 0)
        def _():
            w = wkv_ref[...]
            wlo_s[...] = (w - w.astype(BF16).astype(F32)).astype(BF16)

    @pl.when(i % (L // tm) == 0)
    def _():
        zbuf[0:8, :] = jnp.zeros((8, W_A), F32)
        ccar[...] = jnp.zeros_like(ccar)

    h = _rms(x_ref[...], gmix_ref[...])
    hb = h.astype(BF16)

    def proj(lo, hi):
        return _dot(hb, win_ref[:, lo:hi])

    if precise:
        h_lo = (h - hb.astype(F32)).astype(BF16)

        def proj_kv(lo, hi):
            return (proj(lo, hi) + _dot(h_lo, win_ref[:, lo:hi])
                    + _dot(hb, wlo_s[:, lo - KV_COL0:hi - KV_COL0]))
    else:
        proj_kv = proj

    pa = proj(0, 3 * W_A)
    z = pa[:, W_A:2 * W_A] * pa[:, 2 * W_A:3 * W_A]
    zbuf[8:8 + tm, :] = z
    z1 = zbuf[7:7 + tm, :]
    z2 = zbuf[6:6 + tm, :]
    cw = convw_ref[...]
    y_a = pa[:, 0:W_A] * (cw[0:1, :] * z2 + cw[1:2, :] * z1 + cw[2:3, :] * z)
    tail = z[tm - 2:tm, :]
    ctail_ref[...] = tail
    zbuf[6:8, :] = tail
    gout = gout_ref[...]
    mab_ref[:, 0:W_A] = _rms(y_a, gout[:, 0:W_A]).astype(BF16)

    o = 3 * W_A
    pb = proj(o, o + 2 * W_B)
    u_b = pb[:, 0:W_B]
    v_b = pb[:, W_B:2 * W_B]
    mu = jnp.mean(v_b, axis=-1, keepdims=True)
    vc = v_b - mu
    var = jnp.mean(vc * vc, axis=-1, keepdims=True)
    vn = vc * lax.rsqrt(var + EPS) * lng_ref[...] + lnb_ref[...]
    vnb = vn.astype(BF16)
    r_i = lax.broadcasted_iota(I32, (CHUNK, CHUNK), 0)
    c_i = lax.broadcasted_iota(I32, (CHUNK, CHUNK), 1)
    lane_head = lax.broadcasted_iota(I32, (CHUNK, W_B), 1) // HD
    ws_t = [jnp.where(r_i >= c_i, ws_ref[hh], 0.0).astype(BF16) for hh in range(4)]
    parts = []
    for cidx in range(tm // CHUNK):
        vchunk = vnb[cidx * CHUNK:(cidx + 1) * CHUNK, :]
        sc = jnp.zeros((CHUNK, W_B), F32)
        for hh in range(4):
            sc = jnp.where(lane_head == hh, _dot(ws_t[hh], vchunk), sc)
        parts.append(sc + bse_ref[...])
    s = jnp.concatenate(parts, axis=0)
    y_b = u_b * s
    mab_ref[:, W_A:W_A + W_B] = _rms(y_b, gout[:, W_A:W_A + W_B]).astype(BF16)

    o = 3 * W_A + 2 * W_B
    q = proj(o, o + W_C)
    k = proj_kv(o + W_C, o + 2 * W_C)
    v = proj_kv(o + 2 * W_C, o + 3 * W_C)
    f = proj_kv(o + 3 * W_C, D_IN_PAD)
    qbf_ref[...] = (q * (HD ** -0.5)).astype(BF16)
    k_ref[...] = k
    v_ref[...] = v
    kbf_ref[...] = k.astype(BF16)
    vbf_ref[...] = v.astype(BF16)
    lf = _log_sigmoid(f + bf_ref[...])
    logf_ref[...] = lf[:, 0:H_C]

    p1 = lf.astype(BF16)
    r1 = lf - p1.astype(F32)
    p2 = r1.astype(BF16)
    p3 = (r1 - p2.astype(F32)).astype(BF16)
    tr = lax.broadcasted_iota(I32, (tm, tm), 0) >= lax.broadcasted_iota(I32, (tm, tm), 1)
    trb = jnp.where(tr, 1.0, 0.0).astype(BF16)
    cs = _dot(trb, p1) + _dot(trb, p2) + _dot(trb, p3) + ccar[...]
    c_ref[...] = cs
    ccar[...] = cs[tm - 1:tm, :]


def _prompt_in(x, gmix, win_bf, convw, lng, lnb, ws, bse, bf_pad, gout, win_kv=None):
    tm = TM_A
    n = T // tm
    precise = win_kv is not None
    full = lambda shape: pl.BlockSpec(shape, lambda i: (0,) * len(shape))
    tok = lambda w: pl.BlockSpec((tm, w), lambda i: (i, 0))
    return pl.pallas_call(
        functools.partial(_prompt_in_kernel, precise=precise),
        grid=(n,),
        in_specs=[tok(D), full((1, D)), full((D, D_IN_PAD)), full((3, W_A)), full((1, W_B)),
                  full((1, W_B)), full((4, CHUNK, CHUNK)), full((CHUNK, W_B)), full((1, 128)),
                  full((1, D))] + ([full((D, D_IN_PAD - KV_COL0))] if precise else []),
        out_specs=[tok(W_A + W_B), tok(W_C), tok(W_C), tok(W_C), tok(W_C), tok(W_C), tok(H_C), tok(128),
                   pl.BlockSpec((None, 2, W_A), lambda i: (i // (L // tm), 0, 0))],
        out_shape=[jax.ShapeDtypeStruct((T, W_A + W_B), BF16),
                   jax.ShapeDtypeStruct((T, W_C), BF16),
                   jax.ShapeDtypeStruct((T, W_C), BF16),
                   jax.ShapeDtypeStruct((T, W_C), BF16),
                   jax.ShapeDtypeStruct((T, W_C), F32),
                   jax.ShapeDtypeStruct((T, W_C), F32),
                   jax.ShapeDtypeStruct((T, H_C), F32),
                   jax.ShapeDtypeStruct((T, 128), F32),
                   jax.ShapeDtypeStruct((NB, 2, W_A), F32)],
        scratch_shapes=[pltpu.VMEM((tm + 8, W_A), F32), pltpu.VMEM((1, 128), F32)]
        + ([pltpu.VMEM((D, D_IN_PAD - KV_COL0), BF16)] if precise else []),
        compiler_params=_cparams(("arbitrary",)),
        name="prompt_in",
    )(x, gmix, win_bf, convw, lng, lnb, ws, bse, bf_pad, gout, *((win_kv,) if precise else ()))


def _prompt_attn_kernel(q_ref, k_ref, v_ref, ct_ref, gout_ref, o_ref, yc):
    tq = TQ
    qi = pl.program_id(1)
    left = lax.broadcasted_iota(I32, (1, 128), 1) < HD
    row = lax.broadcasted_iota(I32, (tq, tq), 0)
    col = lax.broadcasted_iota(I32, (tq, tq), 1)
    zero = jnp.zeros((), BF16)

    for hp in range(H_C // 2):
        ls = slice(hp * 128, (hp + 1) * 128)
        q2 = q_ref[:, ls]
        q_a = jnp.where(left, q2, zero)
        q_b = jnp.where(left, zero, q2)
        h_a, h_b = 2 * hp, 2 * hp + 1

        def block(ki, carry, masked):
            m_a, l_a, m_b, l_b, acc = carry
            ks = pl.multiple_of(ki * tq, tq)
            k2 = k_ref[pl.ds(ks, tq), ls]
            v2 = v_ref[pl.ds(ks, tq), ls]

            def one(qh, h, m, l):
                s = _dot_nt(qh, k2) - ct_ref[h:h + 1, pl.ds(ks, tq)]
                if masked:
                    s = jnp.where(col <= row, s, NEG)
                mn = jnp.maximum(m, jnp.max(s, axis=-1, keepdims=True))
                a = jnp.exp(m - mn)
                p = jnp.exp(s - mn)
                ln = a * l + jnp.sum(p, axis=-1, keepdims=True)
                return mn, ln, a, _dot(p.astype(BF16), v2)

            m_a, l_a, a_a, pv_a = one(q_a, h_a, m_a, l_a)
            m_b, l_b, a_b, pv_b = one(q_b, h_b, m_b, l_b)
            acc = acc * jnp.where(left, a_a, a_b) + jnp.where(left, pv_a, pv_b)
            return m_a, l_a, m_b, l_b, acc

        init = (jnp.full((tq, 1), NEG, F32), jnp.zeros((tq, 1), F32),
                jnp.full((tq, 1), NEG, F32), jnp.zeros((tq, 1), F32),
                jnp.zeros((tq, 128), F32))
        carry = lax.fori_loop(0, qi, lambda ki, c: block(ki, c, False), init)
        m_a, l_a, m_b, l_b, acc = block(qi, carry, True)
        yc[:, ls] = acc / jnp.where(left, l_a, l_b)

    o_ref[...] = _rms(yc[...], gout_ref[:, W_A + W_B:]).astype(BF16)


def _prompt_attn(qbf, kbf, vbf, ct, gout):
    nq = L // TQ
    return pl.pallas_call(
        _prompt_attn_kernel,
        grid=(NB, nq),
        in_specs=[pl.BlockSpec((TQ, W_C), lambda n, qi: (n * nq + qi, 0)),
                  pl.BlockSpec((L, W_C), lambda n, qi: (n, 0)),
                  pl.BlockSpec((L, W_C), lambda n, qi: (n, 0)),
                  pl.BlockSpec((None, H_C, L), lambda n, qi: (n, 0, 0)),
                  pl.BlockSpec((1, D), lambda n, qi: (0, 0))],
        out_specs=pl.BlockSpec((TQ, W_C), lambda n, qi: (n * nq + qi, 0)),
        out_shape=jax.ShapeDtypeStruct((T, W_C), BF16),
        scratch_shapes=[pltpu.VMEM((TQ, W_C), F32)],
        compiler_params=_cparams(("arbitrary", "arbitrary")),
        name="prompt_attn",
    )(qbf, kbf, vbf, ct, gout)


TAIL = 8


def _mm3(a_hi, a_lo, b_hi, b_lo):
    return _dot(a_hi, b_hi) + _dot(a_lo, b_hi) + _dot(a_hi, b_lo)


def _tail_proj_kernel(x_ref, gmix_ref, win_ref, convw_ref, lng_ref, lnb_ref, ws_ref, bse_ref, gout_ref,
                      mab_ref, q_ref):
    h_hi, h_lo = _split(_rms(x_ref[...], gmix_ref[...]))

    def proj(lo, hi):
        w_hi, w_lo = _split(win_ref[:, lo:hi])
        return _mm3(h_hi, h_lo, w_hi, w_lo)

    pa = proj(0, 3 * W_A)
    pb = proj(3 * W_A, 3 * W_A + 2 * W_B)
    o = 3 * W_A + 2 * W_B
    q = proj(o, o + W_C) * (HD ** -0.5)
    cw = convw_ref[...]
    gout = gout_ref[...]
    lo8 = CHUNK - TAIL
    r_i = lax.broadcasted_iota(I32, (TAIL, CHUNK), 0) + lo8
    c_i = lax.broadcasted_iota(I32, (TAIL, CHUNK), 1)
    lane_head = lax.broadcasted_iota(I32, (TAIL, W_B), 1) // HD
    ws_t = [_split(jnp.where(c_i <= r_i, ws_ref[hh, lo8:CHUNK, :], 0.0)) for hh in range(4)]
    for n in range(NB):
        r0 = n * CHUNK
        t0 = r0 + lo8
        z = pa[r0:r0 + CHUNK, W_A:2 * W_A] * pa[r0:r0 + CHUNK, 2 * W_A:3 * W_A]
        y_a = pa[t0:t0 + TAIL, 0:W_A] * (cw[0:1, :] * z[lo8 - 2:CHUNK - 2] + cw[1:2, :] * z[lo8 - 1:CHUNK - 1]
                                         + cw[2:3, :] * z[lo8:CHUNK])
        v_b = pb[r0:r0 + CHUNK, W_B:2 * W_B]
        mu = jnp.mean(v_b, axis=-1, keepdims=True)
        vc = v_b - mu
        var = jnp.mean(vc * vc, axis=-1, keepdims=True)
        vn_hi, vn_lo = _split(vc * lax.rsqrt(var + EPS) * lng_ref[...] + lnb_ref[...])
        s = jnp.zeros((TAIL, W_B), F32)
        for hh in range(4):
            s = jnp.where(lane_head == hh, _mm3(ws_t[hh][0], ws_t[hh][1], vn_hi, vn_lo), s)
        y_b = pb[t0:t0 + TAIL, 0:W_B] * (s + bse_ref[lo8:CHUNK, :])
        rows = slice(n * TAIL, (n + 1) * TAIL)
        mab_ref[rows, 0:W_A] = _rms(y_a, gout[:, 0:W_A])
        mab_ref[rows, W_A:W_A + W_B] = _rms(y_b, gout[:, W_A:W_A + W_B])
        q_ref[rows, :] = q[t0:t0 + TAIL, :]


def _tail_proj(x_chunk, gmix, win, convw, lng, lnb, ws, bse, gout):
    return pl.pallas_call(
        _tail_proj_kernel,
        out_shape=[jax.ShapeDtypeStruct((NB * TAIL, W_A + W_B), F32), jax.ShapeDtypeStruct((NB * TAIL, W_C), F32)],
        compiler_params=pltpu.CompilerParams(vmem_limit_bytes=VMEM_LIMIT),
        name="tail_proj",
    )(x_chunk, gmix, win, convw, lng, lnb, ws, bse, gout)


def _tail_attn_kernel(q_ref, k_ref, v_ref, c_ref, gout_ref, mc_ref):
    q8 = q_ref[...]
    lane_head = lax.broadcasted_iota(I32, (TAIL, W_C), 1) // HD
    pieces = [jnp.where(lane_head == h, q8, 0.0) for h in range(H_C)]
    pieces.append(jnp.zeros((128 - H_C * TAIL, W_C), F32))
    q_hi, q_lo = _split(jnp.concatenate(pieces, axis=0))
    k_hi, k_lo = _split(k_ref[...])
    st = _dot_nt(k_hi, q_hi) + _dot_nt(k_lo, q_hi) + _dot_nt(k_hi, q_lo)
    c = c_ref[...]
    c1 = c.astype(BF16)
    r1 = c - c1.astype(F32)
    c2 = r1.astype(BF16)
    c3 = (r1 - c2.astype(F32)).astype(BF16)
    e_r = lax.broadcasted_iota(I32, (128, 128), 0)
    e_c = lax.broadcasted_iota(I32, (128, 128), 1)
    expand = jnp.where((e_c // TAIL == e_r) & (e_r < H_C), 1.0, 0.0).astype(BF16)
    c_keys = _dot(c1, expand) + _dot(c2, expand) + _dot(c3, expand)
    key = lax.broadcasted_iota(I32, (L, 128), 0)
    qpos = L - TAIL + lax.broadcasted_iota(I32, (L, 128), 1) % TAIL
    st = jnp.where(key <= qpos, st - c_keys, NEG)
    p = jnp.exp(st - jnp.max(st, axis=0, keepdims=True))
    pt = p.T
    denom = jnp.sum(pt, axis=1, keepdims=True)
    p_hi, p_lo = _split(pt)
    v_hi, v_lo = _split(v_ref[...])
    o = _mm3(p_hi, p_lo, v_hi, v_lo) / denom
    o8 = jnp.zeros((TAIL, W_C), F32)
    for h in range(H_C):
        o8 = jnp.where(lane_head == h, o[h * TAIL:(h + 1) * TAIL, :], o8)
    mc_ref[...] = _rms(o8, gout_ref[:, W_A + W_B:])


def _tail_attn(q_t, k, v, c128, gout):
    return pl.pallas_call(
        _tail_attn_kernel,
        grid=(NB,),
        in_specs=[pl.BlockSpec((TAIL, W_C), lambda n: (n, 0)),
                  pl.BlockSpec((L, W_C), lambda n: (n, 0)),
                  pl.BlockSpec((L, W_C), lambda n: (n, 0)),
                  pl.BlockSpec((L, 128), lambda n: (n, 0)),
                  pl.BlockSpec((1, D), lambda n: (0, 0))],
        out_specs=pl.BlockSpec((TAIL, W_C), lambda n: (n, 0)),
        out_shape=jax.ShapeDtypeStruct((NB * TAIL, W_C), F32),
        compiler_params=_cparams(("arbitrary",)),
        name="tail_attn",
    )(q_t, k, v, c128, gout)


def _tail_route_kernel(mab_ref, mc_ref, x_ref, wo_ref, gffn_ref, wr_ref, br_ref, lg_ref):
    upd = _mm(mab_ref[...], wo_ref[0:W_A + W_B, :], "x3") + _mm(mc_ref[...], wo_ref[W_A + W_B:, :], "x3")
    h2 = _rms(x_ref[...] + upd, gffn_ref[...])
    lg_ref[...] = _mm(h2, wr_ref[...], "x3") + br_ref[...]


def _tail_route(mab_t, mc_t, x_tail, wo, gffn, wr, br):
    return pl.pallas_call(
        _tail_route_kernel,
        out_shape=jax.ShapeDtypeStruct((NB * TAIL, 128), F32),
        compiler_params=pltpu.CompilerParams(vmem_limit_bytes=VMEM_LIMIT),
        name="tail_route",
    )(mab_t, mc_t, x_tail, wo, gffn, wr, br)


def _route(logits):
    lane = lax.broadcasted_iota(I32, logits.shape, 1).astype(F32)
    big = jnp.float32(1e9)
    gl = jnp.where(lane < NG, logits, NEG)
    gmax = jnp.max(gl, axis=-1, keepdims=True)
    p_g = 1.0 / jnp.sum(jnp.exp(gl - gmax), axis=-1, keepdims=True)
    g_idx = jnp.min(jnp.where(gl == gmax, lane, big), axis=-1, keepdims=True)
    lo = NG + EPG * g_idx
    el = jnp.where((lane >= lo) & (lane < lo + EPG), logits, NEG)
    m1 = jnp.max(el, axis=-1, keepdims=True)
    i1 = jnp.min(jnp.where(el == m1, lane, big), axis=-1, keepdims=True)
    zsum = jnp.sum(jnp.exp(el - m1), axis=-1, keepdims=True)
    el2 = jnp.where(lane == i1, NEG, el)
    m2 = jnp.max(el2, axis=-1, keepdims=True)
    i2 = jnp.min(jnp.where(el2 == m2, lane, big), axis=-1, keepdims=True)
    p1 = 1.0 / zsum
    p2 = jnp.exp(m2 - m1) / zsum
    den = p1 + p2
    w1 = p_g * p1 / den
    w2 = p_g * p2 / den
    e1 = i1 - NG
    e2 = i2 - NG
    info = jnp.where(lane == 0, e1,
                     jnp.where(lane == 1, e2,
                               jnp.where(lane == 2, w1, jnp.where(lane == 3, w2, 0.0))))
    return info, e1, e2


def _out_route_kernel(*refs, mode, tm, use_tail):
    (mab_ref, mc_ref, x_ref, wo_ref, gffn_ref, wr_ref, br_ref, cin_ref) = refs[:8]
    if use_tail:
        tail_ref, xmid_ref, h2_ref, info_ref, cnt_ref, carry, lg_s = refs[8:]
    else:
        xmid_ref, h2_ref, info_ref, cnt_ref, carry = refs[8:]

    @pl.when(pl.program_id(0) == 0)
    def _():
        carry[...] = cin_ref[...]

    upd = _mm(mab_ref[...], wo_ref[0:W_A + W_B, :], mode) + _mm(mc_ref[...], wo_ref[W_A + W_B:, :], mode)
    xm = x_ref[...] + upd
    xmid_ref[...] = xm
    h2 = _rms(xm, gffn_ref[...])
    for c in range(NCH):
        h2_ref[pl.ds(c, tm, stride=NCH), :] = h2[:, c * 128:(c + 1) * 128]
    logits = _mm(h2, wr_ref[...], mode) + br_ref[...]
    if use_tail:
        tiles_per_seq = L // tm
        lg_s[...] = logits

        @pl.when(pl.program_id(0) % tiles_per_seq == tiles_per_seq - 1)
        def _():
            lg_s[tm - TAIL:tm, :] = tail_ref[...]
        logits = lg_s[...]
    info, e1, e2 = _route(logits)

    lane = lax.broadcasted_iota(I32, (tm, 128), 1).astype(F32)
    hit1 = lane == e1
    hit2 = lane == e2
    both = jnp.where(hit1, 1.0, 0.0) + jnp.where(hit2, 1.0, 0.0)
    earlier = lax.broadcasted_iota(I32, (tm, tm), 0) > lax.broadcasted_iota(I32, (tm, tm), 1)
    before = _dot(jnp.where(earlier, 1.0, 0.0).astype(BF16), both.astype(BF16)) + carry[...]
    r1 = jnp.sum(jnp.where(hit1, before, 0.0), axis=-1, keepdims=True)
    r2 = jnp.sum(jnp.where(hit2, before, 0.0), axis=-1, keepdims=True)
    info_ref[...] = jnp.where(lane == 4, r1, jnp.where(lane == 5, r2, info))
    total = carry[...] + jnp.sum(both, axis=0, keepdims=True)
    carry[...] = total
    cnt_ref[...] = total


def _out_route(mab, mc, x, wo, gffn, wr, br, cnt_in, tail=None, *, mode, tm):
    n_tok = x.shape[0]
    use_tail = tail is not None
    full = lambda shape: pl.BlockSpec(shape, lambda i: (0,) * len(shape))
    tok = lambda w: pl.BlockSpec((tm, w), lambda i: (i, 0))
    in_specs = [tok(W_A + W_B), tok(W_C), tok(D), full((D, D)), full((1, D)), full((D, 128)),
                full((1, 128)), full((1, 128))]
    scratch = [pltpu.VMEM((1, 128), F32)]
    args = [mab, mc, x, wo, gffn, wr, br, cnt_in]
    if use_tail:
        in_specs.append(pl.BlockSpec((None, TAIL, 128), lambda i: (i // (L // tm), 0, 0)))
        scratch.append(pltpu.VMEM((tm, 128), F32))
        args.append(tail)
    return pl.pallas_call(
        functools.partial(_out_route_kernel, mode=mode, tm=tm, use_tail=use_tail),
        grid=(n_tok // tm,),
        in_specs=in_specs,
        out_specs=[tok(D), pl.BlockSpec((tm * NCH, 128), lambda i: (i, 0)), tok(128), full((1, 128))],
        out_shape=[jax.ShapeDtypeStruct((n_tok, D), F32),
                   jax.ShapeDtypeStruct((n_tok * NCH, 128), F32),
                   jax.ShapeDtypeStruct((n_tok, 128), F32),
                   jax.ShapeDtypeStruct((1, 128), F32)],
        scratch_shapes=scratch,
        compiler_params=_cparams(("arbitrary",)),
        name="out_route_" + mode,
    )(*args)


TM_D = 512
DISPATCH_UNROLL = 4


def _dispatch_kernel(pos_ref, zrow_ref, h2p_ref, h2s_ref, xs_ref, zbuf, sem, zsem):
    i = pl.program_id(0)
    last = i == pl.num_programs(0) - 1

    @pl.when(i == 0)
    def _():
        zbuf[...] = jnp.zeros_like(zbuf)

        def zero_copy(e):
            return pltpu.make_async_copy(zbuf, xs_ref.at[pl.ds(zrow_ref[e], TMX)], zsem)
        for e in range(NE + N_SPARE):
            pl.when(zrow_ref[e] >= 0)(lambda e=e: zero_copy(e).start())
        for e in range(NE + N_SPARE):
            pl.when(zrow_ref[e] >= 0)(lambda e=e: zero_copy(e).wait())

    def copy_token(src_ref, r, slot):
        for k in range(2):
            pltpu.make_async_copy(src_ref.at[r], xs_ref.at[pos_ref[slot + k]], sem).start()

    def wait_rows(n):
        pltpu.make_async_copy(xs_ref.at[pl.ds(0, n)], xs_ref.at[pl.ds(0, n)], sem).wait()

    def body(j, c):
        for u in range(DISPATCH_UNROLL):
            r = j * DISPATCH_UNROLL + u
            copy_token(h2p_ref, r, 2 * (i * TM_D + r))
        return c
    lax.fori_loop(0, TM_D // DISPATCH_UNROLL, body, 0)

    @pl.when(last)
    def _():
        def sample_body(j, c):
            copy_token(h2s_ref, j, 2 * (T + j))
            return c
        lax.fori_loop(0, NS, sample_body, 0)
        wait_rows(2 * NS)

    wait_rows(2 * TM_D)


def _dispatch(pos, zrow, h2p, h2s):
    grid_spec = pltpu.PrefetchScalarGridSpec(
        num_scalar_prefetch=2,
        grid=(T // TM_D,),
        in_specs=[pl.BlockSpec((TM_D, NCH, 128), lambda i, pos, zrow: (i, 0, 0)),
                  pl.BlockSpec((NS, NCH, 128), lambda i, pos, zrow: (0, 0, 0))],
        out_specs=pl.BlockSpec(memory_space=pl.ANY),
        scratch_shapes=[pltpu.VMEM((TMX, NCH, 128), F32), pltpu.SemaphoreType.DMA(()),
                        pltpu.SemaphoreType.DMA(())],
    )
    return pl.pallas_call(
        _dispatch_kernel,
        grid_spec=grid_spec,
        out_shape=jax.ShapeDtypeStruct((S_PAD, NCH, 128), F32),
        compiler_params=_cparams(("arbitrary",)),
        name="dispatch",
    )(pos, zrow, h2p.reshape(T, NCH, 128), h2s.reshape(NS, NCH, 128))


def _expert_kernel(te_ref, tc_ref, tn_ref, tp_ref, xs_ref, wg_hbm, wu_hbm, wd_hbm, y_ref,
                   wg_f, wu_f, wd_f, wg_b, wu_b, wd_b, sem, *, layer):
    t = pl.program_id(0)
    slot = tp_ref[t]

    def fetch(e, s):
        return (pltpu.make_async_copy(wg_hbm.at[layer * NE + e], wg_f.at[s], sem.at[s, 0]),
                pltpu.make_async_copy(wu_hbm.at[layer * NE + e], wu_f.at[s], sem.at[s, 1]),
                pltpu.make_async_copy(wd_hbm.at[layer * NE + e], wd_f.at[s], sem.at[s, 2]))

    @pl.when(t == 0)
    def _():
        for cp in fetch(te_ref[0], 0):
            cp.start()

    @pl.when((t == 0) | (te_ref[t] != te_ref[jnp.maximum(t - 1, 0)]))
    def _():
        for cp in fetch(te_ref[t], slot):
            cp.wait()

        @pl.when(tn_ref[t] >= 0)
        def _():
            for cp in fetch(tn_ref[t], 1 - slot):
                cp.start()

        wg_b[...] = wg_f[slot].astype(BF16)
        wu_b[...] = wu_f[slot].astype(BF16)
        wd_b[...] = wd_f[slot].astype(BF16)

    @pl.when(tc_ref[t] > 0)
    def _():
        xb = jnp.concatenate([xs_ref[pl.ds(c, TMX, stride=NCH), :] for c in range(NCH)], axis=1).astype(BF16)
        a = _dot(xb, wg_b[...])
        b = _dot(xb, wu_b[...])
        act = (a * _sigmoid(a) * b).astype(BF16)
        y = _dot(act, wd_b[...])
        for c in range(NCH):
            y_ref[pl.ds(c, TMX, stride=NCH), :] = y[:, c * 128:(c + 1) * 128]

    @pl.when(tc_ref[t] == 0)
    def _():
        y_ref[...] = jnp.zeros_like(y_ref)


def _experts(tile_e, tile_cnt, tile_next, tile_par, xs_sorted, wg, wu, wd, layer):
    grid_spec = pltpu.PrefetchScalarGridSpec(
        num_scalar_prefetch=4,
        grid=(NT,),
        in_specs=[pl.BlockSpec((TMX * NCH, 128), lambda t, *_: (t, 0)),
                  pl.BlockSpec(memory_space=pl.ANY), pl.BlockSpec(memory_space=pl.ANY),
                  pl.BlockSpec(memory_space=pl.ANY)],
        out_specs=pl.BlockSpec((TMX * NCH, 128), lambda t, *_: (t, 0)),
        scratch_shapes=[pltpu.VMEM((2, D, DE), F32), pltpu.VMEM((2, D, DE), F32), pltpu.VMEM((2, DE, D), F32),
                        pltpu.VMEM((D, DE), BF16), pltpu.VMEM((D, DE), BF16), pltpu.VMEM((DE, D), BF16),
                        pltpu.SemaphoreType.DMA((2, 3))],
    )
    y = pl.pallas_call(
        functools.partial(_expert_kernel, layer=layer),
        grid_spec=grid_spec,
        out_shape=jax.ShapeDtypeStruct((S_PAD * NCH, 128), F32),
        compiler_params=_cparams(("arbitrary",)),
        name="experts",
    )(tile_e, tile_cnt, tile_next, tile_par, xs_sorted.reshape(S_PAD * NCH, 128), wg, wu, wd)
    return y.reshape(S_PAD, NCH, 128)


def _combine_ple_kernel(pos_ref, xmid_ref, info_ref, p_ref, wple_ref, wpg_ref, gple_ref, gfin_ref, ys_ref,
                        x_ref, y_ref, gbuf, x2s, sem, *, mode, tm, slot0):
    i = pl.program_id(0)
    n = pl.num_programs(0)

    def start_row(tile, buf, j):
        s = slot0 + 2 * (tile * tm + j)
        for k in range(2):
            dst = gbuf.at[buf, pl.ds(pl.multiple_of((k * tm + j) * NCH, NCH), NCH), :]
            pltpu.make_async_copy(ys_ref.at[pos_ref[s + k]], dst, sem.at[buf]).start()

    @pl.when(i == 0)
    def _():
        def body(j, c):
            start_row(0, 0, j)
            return c
        lax.fori_loop(0, tm, body, 0)

    buf = i % 2

    def compute(prefetch):
        nxt = 0
        def issue_some(count):
            nonlocal nxt
            if prefetch:
                for j in range(nxt, min(nxt + count, tm)):
                    start_row(i + 1, 1 - buf, j)
                nxt = min(nxt + count, tm)

        pltpu.make_async_copy(gbuf.at[buf], gbuf.at[buf], sem.at[buf]).wait()
        info = info_ref[...]
        w1 = info[:, 2:3]
        w2 = info[:, 3:4]
        for c in range(8):
            cs = slice(c * 128, (c + 1) * 128)
            y1 = gbuf[buf, pl.ds(c, tm, stride=NCH), :]
            y2 = gbuf[buf, pl.ds(tm * NCH + c, tm, stride=NCH), :]
            x2s[:, cs] = xmid_ref[:, cs] + w1 * y1 + w2 * y2
            issue_some(tm // 16)
        x2 = x2s[...]
        gate = _sigmoid(_mm(_rms(x2, gple_ref[...]), wpg_ref[...], mode))
        issue_some(tm // 4)
        x3 = x2 + _mm(p_ref[...], wple_ref[...], mode) * gate
        issue_some(tm)
        x_ref[...] = x3
        y_ref[...] = _rms(x3, gfin_ref[...])

    @pl.when(i + 1 < n)
    def _():
        compute(True)

    @pl.when(i + 1 == n)
    def _():
        compute(False)


def _combine_ple(pos, xmid, info, p, wple, wpg, gple, gfin, y_sorted, *, mode, tm, slot0):
    n_tok = xmid.shape[0]
    full = lambda shape: pl.BlockSpec(shape, lambda i, pos: (0,) * len(shape))
    tok = lambda w: pl.BlockSpec((tm, w), lambda i, pos: (i, 0))
    grid_spec = pltpu.PrefetchScalarGridSpec(
        num_scalar_prefetch=1,
        grid=(n_tok // tm,),
        in_specs=[tok(D), tok(128), tok(D_PLE), full((D_PLE, D)), full((D, D)), full((1, D)), full((1, D)),
                  pl.BlockSpec(memory_space=pl.ANY)],
        out_specs=[tok(D), tok(D)],
        scratch_shapes=[pltpu.VMEM((2, 2 * tm * NCH, 128), F32), pltpu.VMEM((tm, D), F32),
                        pltpu.SemaphoreType.DMA((2,))],
    )
    return pl.pallas_call(
        functools.partial(_combine_ple_kernel, mode=mode, tm=tm, slot0=slot0),
        grid_spec=grid_spec,
        out_shape=[jax.ShapeDtypeStruct((n_tok, D), F32), jax.ShapeDtypeStruct((n_tok, D), F32)],
        compiler_params=_cparams(("arbitrary",)),
        name="combine_ple_" + mode,
    )(pos, xmid, info, p, wple, wpg, gple, gfin, y_sorted)


def _sample_in_kernel(x_ref, gmix_ref, win_ref, convw_ref, s0_ref, s1_ref, lng_ref, lnb_ref,
                      ws0_ref, bs0_ref, bf_ref, gout_ref,
                      mab_ref, q_ref, k_ref, v_ref, logf_ref, z_ref, vn_ref):
    h = _rms(x_ref[...], gmix_ref[...])
    h_hi, h_lo = _split(h)

    def proj(lo, hi):
        w_hi, w_lo = _split(win_ref[:, lo:hi])
        return _dot(h_hi, w_hi) + _dot(h_lo, w_hi) + _dot(h_hi, w_lo)

    pa = proj(0, 3 * W_A)
    z = pa[:, W_A:2 * W_A] * pa[:, 2 * W_A:3 * W_A]
    cw = convw_ref[...]
    y_a = pa[:, 0:W_A] * (cw[0:1, :] * s0_ref[...] + cw[1:2, :] * s1_ref[...] + cw[2:3, :] * z)
    z_ref[...] = z
    gout = gout_ref[...]
    mab_ref[:, 0:W_A] = _rms(y_a, gout[:, 0:W_A])

    o = 3 * W_A
    pb = proj(o, o + 2 * W_B)
    v_b = pb[:, W_B:2 * W_B]
    mu = jnp.mean(v_b, axis=-1, keepdims=True)
    vc = v_b - mu
    var = jnp.mean(vc * vc, axis=-1, keepdims=True)
    vn = vc * lax.rsqrt(var + EPS) * lng_ref[...] + lnb_ref[...]
    vn_ref[...] = vn
    y_b = pb[:, 0:W_B] * (ws0_ref[...] * vn + bs0_ref[...])
    mab_ref[:, W_A:W_A + W_B] = _rms(y_b, gout[:, W_A:W_A + W_B])

    o = 3 * W_A + 2 * W_B
    q_ref[...] = proj(o, o + W_C) * (HD ** -0.5)
    k_ref[...] = proj(o + W_C, o + 2 * W_C)
    v_ref[...] = proj(o + 2 * W_C, o + 3 * W_C)
    f = proj(o + 3 * W_C, D_IN_PAD)
    logf_ref[...] = _log_sigmoid(f + bf_ref[...])


def _sample_in(x, gmix, win_pad, convw, s0, s1, lng, lnb, ws0, bs0, bf_pad, gout):
    shapes = [(NS, W_A + W_B), (NS, W_C), (NS, W_C), (NS, W_C), (NS, 128), (NS, W_A), (NS, W_B)]
    return pl.pallas_call(
        _sample_in_kernel,
        out_shape=[jax.ShapeDtypeStruct(s, F32) for s in shapes],
        compiler_params=pltpu.CompilerParams(vmem_limit_bytes=VMEM_LIMIT),
        name="sample_in",
    )(x, gmix, win_pad, convw, s0, s1, lng, lnb, ws0, bs0, bf_pad, gout)


def _sample_attn_kernel(pt_ref, qb_ref, q8_ref, kn_ref, vnt_ref, lfn_ref, gct_ref, *rest):
    k_refs = rest[0:PP]
    v_refs = rest[PP:2 * PP]
    lf_refs = rest[2 * PP:3 * PP]
    o_ref = rest[3 * PP]
    m_s, l_s, acc_s, tot_s = rest[3 * PP + 1:]
    j = pl.program_id(1)

    @pl.when(j == 0)
    def _():
        m_s[...] = jnp.full_like(m_s, NEG)
        l_s[...] = jnp.zeros_like(l_s)
        acc_s[...] = jnp.zeros_like(acc_s)
        tot_s[...] = jnp.zeros_like(tot_s)

    cn = lfn_ref[...]
    lane = lax.broadcasted_iota(I32, (H_C, PAGE), 1)

    tot = tot_s[...]
    scores = []
    for pp in reversed(range(PP)):
        lf = lf_refs[pp][...]
        inc = lf
        sh = 1
        while sh < PAGE:
            inc = inc + jnp.where(lane >= sh, pltpu.roll(inc, sh, axis=1), 0.0)
            sh *= 2
        page_tot = inc[:, PAGE - 1:PAGE]
        bias = (page_tot - inc) + tot + cn
        tot = tot + page_tot
        rows = [jnp.sum(k_refs[pp][h] * qb_ref[h], axis=0, keepdims=True) for h in range(H_C)]
        scores.append((pp, jnp.concatenate(rows, axis=0) + bias))
    tot_s[...] = tot

    m_old = m_s[...]
    m = m_old
    for _, s in scores:
        m = jnp.maximum(m, jnp.max(s, axis=-1, keepdims=True))
    a = jnp.exp(m_old - m)
    l = a * l_s[...]
    probs = []
    for pp, s in scores:
        p = jnp.exp(s - m)
        l = l + jnp.sum(p, axis=-1, keepdims=True)
        probs.append((pp, p))
    m_s[...] = m
    l_s[...] = l
    for h in range(H_C):
        acc_h = acc_s[h] * a[h:h + 1, :]
        for pp, p in probs:
            acc_h = acc_h + v_refs[pp][h] * p[h:h + 1, :]
        acc_s[h] = acc_h

    @pl.when(j == NPG - 1)
    def _():
        s_new = jnp.sum(q8_ref[...] * kn_ref[...], axis=-1, keepdims=True)
        mn = jnp.maximum(m, s_new)
        a2 = jnp.exp(m - mn)
        pn = jnp.exp(s_new - mn)
        lt = a2 * l + pn
        outs = []
        ms = jnp.zeros((1, 1), F32)
        for h in range(H_C):
            hs = slice(h, h + 1)
            o_h = (a2[hs, :] * jnp.sum(acc_s[h], axis=1, keepdims=True) + pn[hs, :] * vnt_ref[h]) / lt[hs, :]
            ms = ms + jnp.sum(o_h * o_h, axis=0, keepdims=True)
            outs.append(o_h)
        scale = lax.rsqrt(ms / W_C + EPS)
        for h in range(H_C):
            o_ref[h] = outs[h] * scale * gct_ref[h]


def _sample_attn(pt_flat, qb, q8, kn8, vnt, lfn3, gct, ckt, cvt, clft, layer):
    def page_map(nd):
        return lambda pp: (lambda n, j, pt: (layer, pt[n * N_PAGES + (NPG - 1 - j) * PP + pp]) + (0,) * nd)

    head = pl.BlockSpec((None, H_C, HD), lambda n, j, pt: (n, 0, 0))
    col = pl.BlockSpec((None, H_C, HD, 1), lambda n, j, pt: (n, 0, 0, 0))
    in_specs = [pl.BlockSpec((None, H_C, HD, PAGE), lambda n, j, pt: (n, 0, 0, 0)),
                head, head, col,
                pl.BlockSpec((None, H_C, 1), lambda n, j, pt: (n, 0, 0)),
                pl.BlockSpec((H_C, HD, 1), lambda n, j, pt: (0, 0, 0))]
    in_specs += [pl.BlockSpec((None, None, H_C, HD, PAGE), page_map(3)(pp)) for pp in range(PP)]
    in_specs += [pl.BlockSpec((None, None, H_C, HD, PAGE), page_map(3)(pp)) for pp in range(PP)]
    in_specs += [pl.BlockSpec((None, None, H_C, PAGE), page_map(2)(pp)) for pp in range(PP)]
    grid_spec = pltpu.PrefetchScalarGridSpec(
        num_scalar_prefetch=1,
        grid=(NS, NPG),
        in_specs=in_specs,
        out_specs=col,
        scratch_shapes=[pltpu.VMEM((H_C, 1), F32), pltpu.VMEM((H_C, 1), F32),
                        pltpu.VMEM((H_C, HD, PAGE), F32), pltpu.VMEM((H_C, 1), F32)],
    )
    return pl.pallas_call(
        _sample_attn_kernel,
        grid_spec=grid_spec,
        out_shape=jax.ShapeDtypeStruct((NS, H_C, HD, 1), F32),
        compiler_params=_cparams(("arbitrary", "arbitrary")),
        name="sample_attn",
    )(pt_flat, qb, q8, kn8, vnt, lfn3, gct, *([ckt] * PP), *([cvt] * PP), *([clft] * PP))


def _schedule(cnt_f, info_all):
    cnt = cnt_f[0, :NE].astype(I32)
    ntile = (cnt + TMX - 1) // TMX
    tcum = jnp.cumsum(ntile).astype(I32)
    tbase = tcum - ntile
    t = jnp.arange(NT, dtype=I32)
    ids = jnp.arange(NE, dtype=I32)
    last_e = jnp.max(jnp.where(ntile > 0, ids, 0))
    tile_e = jnp.minimum(jnp.sum((tcum[None, :] <= t[:, None]).astype(I32), axis=1), last_e)
    later = (ids[None, :] > ids[:, None]) & (ntile[None, :] > 0)
    next_e = jnp.min(jnp.where(later, ids[None, :], NE), axis=1)
    next_e = jnp.where(next_e < NE, next_e, -1)
    run = jnp.cumsum((ntile > 0).astype(I32)) - 1
    sel = tile_e[:, None] == ids[None, :]
    tile_next = jnp.sum(jnp.where(sel, next_e[None, :], 0), axis=1).astype(I32)
    tile_par = (jnp.sum(jnp.where(sel, run[None, :], 0), axis=1) % 2).astype(I32)
    within = t - jnp.sum(jnp.where(sel, tbase[None, :], 0), axis=1)
    tile_cnt = jnp.clip(jnp.sum(jnp.where(sel, cnt[None, :], 0), axis=1) - within * TMX, 0, TMX).astype(I32)
    eid = info_all[:, 0:2].astype(I32)
    rank = info_all[:, 4:6].astype(I32)
    base = jnp.sum(jnp.where(eid[:, :, None] == jnp.arange(NE, dtype=I32)[None, None, :],
                             (tbase * TMX)[None, None, :], 0), axis=-1)
    pos = (base + rank).reshape(-1).astype(I32)
    last_tile = jnp.where(ntile > 0, tcum - 1, -1)
    spare = tcum[NE - 1] + jnp.arange(N_SPARE, dtype=I32)
    ztile = jnp.concatenate([last_tile, jnp.where(spare < NT, spare, -1)])
    zrow = jnp.where(ztile >= 0, ztile * TMX, -1).astype(I32)
    return (tile_e, tile_cnt, tile_next, tile_par), pos, zrow


def kernel(x_prompt, x_sample, p_prompt, p_sample, cache_k, cache_v, cache_logf, state_conv, page_table, g_mix, w_in, conv_w, ln_g, ln_b, w_s, b_s, b_f, g_out, w_o, g_ffn, w_grp, b_grp, w_rt, b_rt, w_gate, w_up, w_down, g_ple, w_ple, w_ple_gate, g_final):
    n_pool = cache_k.shape[1]
    xp = x_prompt.reshape(T, D)
    xs = x_sample.reshape(NS, D)
    pt_flat = page_table.reshape(-1).astype(I32)
    del n_pool
    ckt = jnp.transpose(cache_k, (0, 1, 3, 4, 2))
    cvt = jnp.transpose(cache_v, (0, 1, 3, 4, 2))
    clft = jnp.swapaxes(cache_logf, 2, 3)
    zero_cnt = jnp.zeros((1, 128), F32)
    wg_all = w_gate.reshape(DEPTH * NE, D, DE)
    wu_all = w_up.reshape(DEPTH * NE, D, DE)
    wd_all = w_down.reshape(DEPTH * NE, DE, D)
    gfin = g_final.reshape(1, D)

    outs = {k: [] for k in ("kp", "vp", "lfp", "cvp", "ks", "vs", "lfs", "cvs", "chv")}
    yp = ys = None
    for i in range(DEPTH):
        gmix = g_mix[i].reshape(1, D)
        gout = g_out[i].reshape(1, D)
        gffn = g_ffn[i].reshape(1, D)
        gple = g_ple[i].reshape(1, D)
        win_pad = jnp.pad(w_in[i], ((0, 0), (0, D_IN_PAD - D_IN)))
        win_bf = win_pad.astype(BF16)
        bf_pad = jnp.pad(b_f[i], (0, 128 - H_C)).reshape(1, 128)
        lng = ln_g[i].reshape(1, W_B)
        lnb = ln_b[i].reshape(1, W_B)
        bse = jnp.repeat(b_s[i].T, HD, axis=1)
        ws0 = jnp.repeat(w_s[i][:, 0, 0], HD).reshape(1, W_B)
        bs0 = jnp.repeat(b_s[i][:, 0], HD).reshape(1, W_B)
        wr = jnp.pad(jnp.concatenate([w_grp[i], w_rt[i].reshape(D, NE)], axis=1), ((0, 0), (0, 128 - NG - NE)))
        br = jnp.pad(jnp.concatenate([b_grp[i], b_rt[i].reshape(NE)]), (0, 128 - NG - NE)).reshape(1, 128)

        first = i == 0
        mab, qbf, kbf, vbf, k_p, v_p, lf_p, c_p, ctail = _prompt_in(
            xp, gmix, win_bf, conv_w[i], lng, lnb, w_s[i], bse, bf_pad, gout,
            win_kv=win_pad[:, KV_COL0:] if first else None)
        ct = jnp.swapaxes(c_p[:, :H_C].reshape(NB, L, H_C), 1, 2)
        mc = _prompt_attn(qbf, kbf, vbf, ct, gout)
        tail = None
        if first:
            x_seq = xp.reshape(NB, L, D)
            mab_t, q_t = _tail_proj(x_seq[:, L - CHUNK:].reshape(NB * CHUNK, D), gmix,
                                    win_pad[:, :3 * W_A + 2 * W_B + W_C], conv_w[i], lng, lnb, w_s[i], bse, gout)
            mc_t = _tail_attn(q_t, k_p, v_p, c_p, gout)
            tail = _tail_route(mab_t, mc_t, x_seq[:, L - TAIL:].reshape(NB * TAIL, D), w_o[i], gffn, wr,
                               br).reshape(NB, TAIL, 128)
        xmid_p, h2_p, info_p, cnt_p = _out_route(mab, mc, xp, w_o[i].astype(BF16), gffn, wr.astype(BF16), br,
                                                 zero_cnt, tail, mode="bf16", tm=TM_C)

        mab_s, q_s, k_s, v_s, lf_s128, z_s, vn_s = _sample_in(
            xs, gmix, win_pad, conv_w[i], state_conv[i, :, 0], state_conv[i, :, 1], lng, lnb, ws0, bs0,
            bf_pad, gout)
        lf_s = lf_s128[:, :H_C]
        q8 = q_s.reshape(NS, H_C, HD)
        mc_s = _sample_attn(pt_flat, jnp.broadcast_to(q8[..., None], (NS, H_C, HD, PAGE)), q8,
                            k_s.reshape(NS, H_C, HD), v_s.reshape(NS, H_C, HD, 1), lf_s.reshape(NS, H_C, 1),
                            gout[:, W_A + W_B:].reshape(H_C, HD, 1), ckt, cvt, clft, i)
        xmid_s, h2_s, info_s, cnt_all = _out_route(mab_s, mc_s.reshape(NS, W_C), xs, w_o[i], gffn, wr, br,
                                                   cnt_p, mode="x3", tm=NS)

        tiles, pos, zrow = _schedule(cnt_all, jnp.concatenate([info_p[:, :8], info_s[:, :8]], axis=0))
        xs_sorted = _dispatch(pos, zrow, h2_p, h2_s)
        y_sorted = _experts(*tiles, xs_sorted, wg_all, wu_all, wd_all, i)

        xp, yp = _combine_ple(pos, xmid_p, info_p, p_prompt[i].reshape(T, D_PLE), w_ple[i].astype(BF16),
                              w_ple_gate[i].astype(BF16), gple, gfin, y_sorted, mode="bf16", tm=TM_F, slot0=0)
        xs, ys = _combine_ple(pos, xmid_s, info_s, p_sample[i].reshape(NS, D_PLE), w_ple[i],
                              w_ple_gate[i], gple, gfin, y_sorted, mode="x3", tm=NS, slot0=2 * T)

        outs["kp"].append(k_p.reshape(NB, L, H_C, HD))
        outs["vp"].append(v_p.reshape(NB, L, H_C, HD))
        outs["lfp"].append(lf_p.reshape(NB, L, H_C))
        outs["cvp"].append(ctail)
        outs["ks"].append(k_s.reshape(NS, 1, H_C, HD))
        outs["vs"].append(v_s.reshape(NS, 1, H_C, HD))
        outs["lfs"].append(lf_s.reshape(NS, 1, H_C))
        outs["cvs"].append(jnp.stack([state_conv[i, :, 1], z_s], axis=1))
        outs["chv"].append(vn_s.reshape(NS, 1, W_B))

    st = lambda k: jnp.stack(outs[k])
    return (yp.reshape(NB, L, D), ys.reshape(NS, 1, D), st("kp"), st("vp"), st("lfp"), st("cvp"),
            st("ks"), st("vs"), st("lfs"), st("cvs"), st("chv"))
```

```python
import functools

import jax
import jax.numpy as jnp
from jax import lax
from jax.experimental import pallas as pl
from jax.experimental.pallas import tpu as pltpu

F32 = jnp.float32
BF16 = jnp.bfloat16
I32 = jnp.int32

D = 1024
NB = 8
L = 2048
T = NB * L
DEPTH = 2
NS = 32
PAGE = 128
N_PAGES = 64
H_C = 8
HD = 64
W_A = 256
W_B = 256
W_C = 512
D_IN = 2824
D_IN_PAD = 2944
KV_COL0 = 3 * W_A + 2 * W_B + W_C
NG = 4
EPG = 8
NE = NG * EPG
DE = 512
D_PLE = 256
CHUNK = 128
EPS = 1e-6
NEG = -jnp.inf

NCH = D // 128
T_ALL = T + NS
S_SLOTS = 2 * T_ALL
TMX = 256
NT = S_SLOTS // TMX + NE
S_PAD = NT * TMX
N_SPARE = NT - (S_SLOTS + TMX - 1) // TMX

TM_A = 512
TQ = 512
TM_C = 512
TM_F = 512
PP = 32
NPG = N_PAGES // PP

VMEM_LIMIT = 56 * 1024 * 1024


def _cparams(sem):
    return pltpu.CompilerParams(dimension_semantics=sem, vmem_limit_bytes=VMEM_LIMIT)


def _split(a):
    hi = a.astype(BF16)
    lo = (a - hi.astype(F32)).astype(BF16)
    return hi, lo


def _dot(a, b):
    return jnp.dot(a, b, preferred_element_type=F32)


def _dot_nt(a, b):
    return lax.dot_general(a, b, (((1,), (1,)), ((), ())), preferred_element_type=F32)


def _mm(a, b, mode):
    if mode == "bf16":
        return _dot(a.astype(BF16), b.astype(BF16))
    a_hi, a_lo = _split(a)
    b_hi, b_lo = _split(b)
    return _dot(a_hi, b_hi) + _dot(a_lo, b_hi) + _dot(a_hi, b_lo)


def _rms(x, g):
    return x * lax.rsqrt(jnp.mean(x * x, axis=-1, keepdims=True) + EPS) * g


def _log_sigmoid(x):
    return jnp.minimum(x, 0.0) - jnp.log1p(jnp.exp(-jnp.abs(x)))


def _sigmoid(x):
    return 1.0 / (1.0 + jnp.exp(-x))


def _prompt_in_kernel(*refs, precise):
    (x_ref, gmix_ref, win_ref, convw_ref, lng_ref, lnb_ref, ws_ref, bse_ref, bf_ref, gout_ref) = refs[:10]
    refs = refs[10:]
    if precise:
        wkv_ref, refs = refs[0], refs[1:]
        wlo_s, refs = refs[-1], refs[:-1]
    (mab_ref, qbf_ref, kbf_ref, vbf_ref, k_ref, v_ref, logf_ref, c_ref, ctail_ref, zbuf, ccar) = refs
    tm = TM_A
    i = pl.program_id(0)

    if precise:
        @pl.when(i == 0)
        def _():
            w = wkv_ref[...]
            wlo_s[...] = (w - w.astype(BF16).astype(F32)).astype(BF16)

    @pl.when(i % (L // tm) == 0)
    def _():
        zbuf[0:8, :] = jnp.zeros((8, W_A), F32)
        ccar[...] = jnp.zeros_like(ccar)

    h = _rms(x_ref[...], gmix_ref[...])
    hb = h.astype(BF16)

    def proj(lo, hi):
        return _dot(hb, win_ref[:, lo:hi])

    if precise:
        h_lo = (h - hb.astype(F32)).astype(BF16)

        def proj_kv(lo, hi):
            return (proj(lo, hi) + _dot(h_lo, win_ref[:, lo:hi])
                    + _dot(hb, wlo_s[:, lo - KV_COL0:hi - KV_COL0]))
    else:
        proj_kv = proj

    pa = proj(0, 3 * W_A)
    z = pa[:, W_A:2 * W_A] * pa[:, 2 * W_A:3 * W_A]
    zbuf[8:8 + tm, :] = z
    z1 = zbuf[7:7 + tm, :]
    z2 = zbuf[6:6 + tm, :]
    cw = convw_ref[...]
    y_a = pa[:, 0:W_A] * (cw[0:1, :] * z2 + cw[1:2, :] * z1 + cw[2:3, :] * z)
    tail = z[tm - 2:tm, :]
    ctail_ref[...] = tail
    zbuf[6:8, :] = tail
    gout = gout_ref[...]
    mab_ref[:, 0:W_A] = _rms(y_a, gout[:, 0:W_A]).astype(BF16)

    o = 3 * W_A
    pb = proj(o, o + 2 * W_B)
    u_b = pb[:, 0:W_B]
    v_b = pb[:, W_B:2 * W_B]
    mu = jnp.mean(v_b, axis=-1, keepdims=True)
    vc = v_b - mu
    var = jnp.mean(vc * vc, axis=-1, keepdims=True)
    vn = vc * lax.rsqrt(var + EPS) * lng_ref[...] + lnb_ref[...]
    vnb = vn.astype(BF16)
    r_i = lax.broadcasted_iota(I32, (CHUNK, CHUNK), 0)
    c_i = lax.broadcasted_iota(I32, (CHUNK, CHUNK), 1)
    lane_head = lax.broadcasted_iota(I32, (CHUNK, W_B), 1) // HD
    ws_t = [jnp.where(r_i >= c_i, ws_ref[hh], 0.0).astype(BF16) for hh in range(4)]
    parts = []
    for cidx in range(tm // CHUNK):
        vchunk = vnb[cidx * CHUNK:(cidx + 1) * CHUNK, :]
        sc = jnp.zeros((CHUNK, W_B), F32)
        for hh in range(4):
            sc = jnp.where(lane_head == hh, _dot(ws_t[hh], vchunk), sc)
        parts.append(sc + bse_ref[...])
    s = jnp.concatenate(parts, axis=0)
    y_b = u_b * s
    mab_ref[:, W_A:W_A + W_B] = _rms(y_b, gout[:, W_A:W_A + W_B]).astype(BF16)

    o = 3 * W_A + 2 * W_B
    q = proj(o, o + W_C)
    k = proj_kv(o + W_C, o + 2 * W_C)
    v = proj_kv(o + 2 * W_C, o + 3 * W_C)
    f = proj_kv(o + 3 * W_C, D_IN_PAD)
    qbf_ref[...] = (q * (HD ** -0.5)).astype(BF16)
    k_ref[...] = k
    v_ref[...] = v
    kbf_ref[...] = k.astype(BF16)
    vbf_ref[...] = v.astype(BF16)
    lf = _log_sigmoid(f + bf_ref[...])
    logf_ref[...] = lf[:, 0:H_C]

    p1 = lf.astype(BF16)
    r1 = lf - p1.astype(F32)
    p2 = r1.astype(BF16)
    p3 = (r1 - p2.astype(F32)).astype(BF16)
    tr = lax.broadcasted_iota(I32, (tm, tm), 0) >= lax.broadcasted_iota(I32, (tm, tm), 1)
    trb = jnp.where(tr, 1.0, 0.0).astype(BF16)
    cs = _dot(trb, p1) + _dot(trb, p2) + _dot(trb, p3) + ccar[...]
    c_ref[...] = cs
    ccar[...] = cs[tm - 1:tm, :]


def _prompt_in(x, gmix, win_bf, convw, lng, lnb, ws, bse, bf_pad, gout, win_kv=None):
    tm = TM_A
    n = T // tm
    precise = win_kv is not None
    full = lambda shape: pl.BlockSpec(shape, lambda i: (0,) * len(shape))
    tok = lambda w: pl.BlockSpec((tm, w), lambda i: (i, 0))
    return pl.pallas_call(
        functools.partial(_prompt_in_kernel, precise=precise),
        grid=(n,),
        in_specs=[tok(D), full((1, D)), full((D, D_IN_PAD)), full((3, W_A)), full((1, W_B)),
                  full((1, W_B)), full((4, CHUNK, CHUNK)), full((CHUNK, W_B)), full((1, 128)),
                  full((1, D))] + ([full((D, D_IN_PAD - KV_COL0))] if precise else []),
        out_specs=[tok(W_A + W_B), tok(W_C), tok(W_C), tok(W_C), tok(W_C), tok(W_C), tok(H_C), tok(128),
                   pl.BlockSpec((None, 2, W_A), lambda i: (i // (L // tm), 0, 0))],
        out_shape=[jax.ShapeDtypeStruct((T, W_A + W_B), BF16),
                   jax.ShapeDtypeStruct((T, W_C), BF16),
                   jax.ShapeDtypeStruct((T, W_C), BF16),
                   jax.ShapeDtypeStruct((T, W_C), BF16),
                   jax.ShapeDtypeStruct((T, W_C), F32),
                   jax.ShapeDtypeStruct((T, W_C), F32),
                   jax.ShapeDtypeStruct((T, H_C), F32),
                   jax.ShapeDtypeStruct((T, 128), F32),
                   jax.ShapeDtypeStruct((NB, 2, W_A), F32)],
        scratch_shapes=[pltpu.VMEM((tm + 8, W_A), F32), pltpu.VMEM((1, 128), F32)]
        + ([pltpu.VMEM((D, D_IN_PAD - KV_COL0), BF16)] if precise else []),
        compiler_params=_cparams(("arbitrary",)),
        name="prompt_in",
    )(x, gmix, win_bf, convw, lng, lnb, ws, bse, bf_pad, gout, *((win_kv,) if precise else ()))


def _prompt_attn_kernel(q_ref, k_ref, v_ref, ct_ref, gout_ref, o_ref, yc):
    tq = TQ
    qi = pl.program_id(1)
    left = lax.broadcasted_iota(I32, (1, 128), 1) < HD
    row = lax.broadcasted_iota(I32, (tq, tq), 0)
    col = lax.broadcasted_iota(I32, (tq, tq), 1)
    zero = jnp.zeros((), BF16)

    for hp in range(H_C // 2):
        ls = slice(hp * 128, (hp + 1) * 128)
        q2 = q_ref[:, ls]
        q_a = jnp.where(left, q2, zero)
        q_b = jnp.where(left, zero, q2)
        h_a, h_b = 2 * hp, 2 * hp + 1

        def block(ki, carry, masked):
            m_a, l_a, m_b, l_b, acc = carry
            ks = pl.multiple_of(ki * tq, tq)
            k2 = k_ref[pl.ds(ks, tq), ls]
            v2 = v_ref[pl.ds(ks, tq), ls]

            def one(qh, h, m, l):
                s = _dot_nt(qh, k2) - ct_ref[h:h + 1, pl.ds(ks, tq)]
                if masked:
                    s = jnp.where(col <= row, s, NEG)
                mn = jnp.maximum(m, jnp.max(s, axis=-1, keepdims=True))
                a = jnp.exp(m - mn)
                p = jnp.exp(s - mn)
                ln = a * l + jnp.sum(p, axis=-1, keepdims=True)
                return mn, ln, a, _dot(p.astype(BF16), v2)

            m_a, l_a, a_a, pv_a = one(q_a, h_a, m_a, l_a)
            m_b, l_b, a_b, pv_b = one(q_b, h_b, m_b, l_b)
            acc = acc * jnp.where(left, a_a, a_b) + jnp.where(left, pv_a, pv_b)
            return m_a, l_a, m_b, l_b, acc

        init = (jnp.full((tq, 1), NEG, F32), jnp.zeros((tq, 1), F32),
                jnp.full((tq, 1), NEG, F32), jnp.zeros((tq, 1), F32),
                jnp.zeros((tq, 128), F32))
        carry = lax.fori_loop(0, qi, lambda ki, c: block(ki, c, False), init)
        m_a, l_a, m_b, l_b, acc = block(qi, carry, True)
        yc[:, ls] = acc / jnp.where(left, l_a, l_b)

    o_ref[...] = _rms(yc[...], gout_ref[:, W_A + W_B:]).astype(BF16)


def _prompt_attn(qbf, kbf, vbf, ct, gout):
    nq = L // TQ
    return pl.pallas_call(
        _prompt_attn_kernel,
        grid=(NB, nq),
        in_specs=[pl.BlockSpec((TQ, W_C), lambda n, qi: (n * nq + qi, 0)),
                  pl.BlockSpec((L, W_C), lambda n, qi: (n, 0)),
                  pl.BlockSpec((L, W_C), lambda n, qi: (n, 0)),
                  pl.BlockSpec((None, H_C, L), lambda n, qi: (n, 0, 0)),
                  pl.BlockSpec((1, D), lambda n, qi: (0, 0))],
        out_specs=pl.BlockSpec((TQ, W_C), lambda n, qi: (n * nq + qi, 0)),
        out_shape=jax.ShapeDtypeStruct((T, W_C), BF16),
        scratch_shapes=[pltpu.VMEM((TQ, W_C), F32)],
        compiler_params=_cparams(("arbitrary", "arbitrary")),
        name="prompt_attn",
    )(qbf, kbf, vbf, ct, gout)


TAIL = 8


def _mm3(a_hi, a_lo, b_hi, b_lo):
    return _dot(a_hi, b_hi) + _dot(a_lo, b_hi) + _dot(a_hi, b_lo)


def _tail_proj_kernel(x_ref, gmix_ref, win_ref, convw_ref, lng_ref, lnb_ref, ws_ref, bse_ref, gout_ref,
                      mab_ref, q_ref):
    h_hi, h_lo = _split(_rms(x_ref[...], gmix_ref[...]))

    def proj(lo, hi):
        w_hi, w_lo = _split(win_ref[:, lo:hi])
        return _mm3(h_hi, h_lo, w_hi, w_lo)

    pa = proj(0, 3 * W_A)
    pb = proj(3 * W_A, 3 * W_A + 2 * W_B)
    o = 3 * W_A + 2 * W_B
    q = proj(o, o + W_C) * (HD ** -0.5)
    cw = convw_ref[...]
    gout = gout_ref[...]
    lo8 = CHUNK - TAIL
    r_i = lax.broadcasted_iota(I32, (TAIL, CHUNK), 0) + lo8
    c_i = lax.broadcasted_iota(I32, (TAIL, CHUNK), 1)
    lane_head = lax.broadcasted_iota(I32, (TAIL, W_B), 1) // HD
    ws_t = [_split(jnp.where(c_i <= r_i, ws_ref[hh, lo8:CHUNK, :], 0.0)) for hh in range(4)]
    for n in range(NB):
        r0 = n * CHUNK
        t0 = r0 + lo8
        z = pa[r0:r0 + CHUNK, W_A:2 * W_A] * pa[r0:r0 + CHUNK, 2 * W_A:3 * W_A]
        y_a = pa[t0:t0 + TAIL, 0:W_A] * (cw[0:1, :] * z[lo8 - 2:CHUNK - 2] + cw[1:2, :] * z[lo8 - 1:CHUNK - 1]
                                         + cw[2:3, :] * z[lo8:CHUNK])
        v_b = pb[r0:r0 + CHUNK, W_B:2 * W_B]
        mu = jnp.mean(v_b, axis=-1, keepdims=True)
        vc = v_b - mu
        var = jnp.mean(vc * vc, axis=-1, keepdims=True)
        vn_hi, vn_lo = _split(vc * lax.rsqrt(var + EPS) * lng_ref[...] + lnb_ref[...])
        s = jnp.zeros((TAIL, W_B), F32)
        for hh in range(4):
            s = jnp.where(lane_head == hh, _mm3(ws_t[hh][0], ws_t[hh][1], vn_hi, vn_lo), s)
        y_b = pb[t0:t0 + TAIL, 0:W_B] * (s + bse_ref[lo8:CHUNK, :])
        rows = slice(n * TAIL, (n + 1) * TAIL)
        mab_ref[rows, 0:W_A] = _rms(y_a, gout[:, 0:W_A])
        mab_ref[rows, W_A:W_A + W_B] = _rms(y_b, gout[:, W_A:W_A + W_B])
        q_ref[rows, :] = q[t0:t0 + TAIL, :]


def _tail_proj(x_chunk, gmix, win, convw, lng, lnb, ws, bse, gout):
    return pl.pallas_call(
        _tail_proj_kernel,
        out_shape=[jax.ShapeDtypeStruct((NB * TAIL, W_A + W_B), F32), jax.ShapeDtypeStruct((NB * TAIL, W_C), F32)],
        compiler_params=pltpu.CompilerParams(vmem_limit_bytes=VMEM_LIMIT),
        name="tail_proj",
    )(x_chunk, gmix, win, convw, lng, lnb, ws, bse, gout)


def _tail_attn_kernel(q_ref, k_ref, v_ref, c_ref, gout_ref, mc_ref):
    q8 = q_ref[...]
    lane_head = lax.broadcasted_iota(I32, (TAIL, W_C), 1) // HD
    pieces = [jnp.where(lane_head == h, q8, 0.0) for h in range(H_C)]
    pieces.append(jnp.zeros((128 - H_C * TAIL, W_C), F32))
    q_hi, q_lo = _split(jnp.concatenate(pieces, axis=0))
    k_hi, k_lo = _split(k_ref[...])
    st = _dot_nt(k_hi, q_hi) + _dot_nt(k_lo, q_hi) + _dot_nt(k_hi, q_lo)
    c = c_ref[...]
    c1 = c.astype(BF16)
    r1 = c - c1.astype(F32)
    c2 = r1.astype(BF16)
    c3 = (r1 - c2.astype(F32)).astype(BF16)
    e_r = lax.broadcasted_iota(I32, (128, 128), 0)
    e_c = lax.broadcasted_iota(I32, (128, 128), 1)
    expand = jnp.where((e_c // TAIL == e_r) & (e_r < H_C), 1.0, 0.0).astype(BF16)
    c_keys = _dot(c1, expand) + _dot(c2, expand) + _dot(c3, expand)
    key = lax.broadcasted_iota(I32, (L, 128), 0)
    qpos = L - TAIL + lax.broadcasted_iota(I32, (L, 128), 1) % TAIL
    st = jnp.where(key <= qpos, st - c_keys, NEG)
    p = jnp.exp(st - jnp.max(st, axis=0, keepdims=True))
    pt = p.T
    denom = jnp.sum(pt, axis=1, keepdims=True)
    p_hi, p_lo = _split(pt)
    v_hi, v_lo = _split(v_ref[...])
    o = _mm3(p_hi, p_lo, v_hi, v_lo) / denom
    o8 = jnp.zeros((TAIL, W_C), F32)
    for h in range(H_C):
        o8 = jnp.where(lane_head == h, o[h * TAIL:(h + 1) * TAIL, :], o8)
    mc_ref[...] = _rms(o8, gout_ref[:, W_A + W_B:])


def _tail_attn(q_t, k, v, c128, gout):
    return pl.pallas_call(
        _tail_attn_kernel,
        grid=(NB,),
        in_specs=[pl.BlockSpec((TAIL, W_C), lambda n: (n, 0)),
                  pl.BlockSpec((L, W_C), lambda n: (n, 0)),
                  pl.BlockSpec((L, W_C), lambda n: (n, 0)),
                  pl.BlockSpec((L, 128), lambda n: (n, 0)),
                  pl.BlockSpec((1, D), lambda n: (0, 0))],
        out_specs=pl.BlockSpec((TAIL, W_C), lambda n: (n, 0)),
        out_shape=jax.ShapeDtypeStruct((NB * TAIL, W_C), F32),
        compiler_params=_cparams(("arbitrary",)),
        name="tail_attn",
    )(q_t, k, v, c128, gout)


def _tail_route_kernel(mab_ref, mc_ref, x_ref, wo_ref, gffn_ref, wr_ref, br_ref, lg_ref):
    upd = _mm(mab_ref[...], wo_ref[0:W_A + W_B, :], "x3") + _mm(mc_ref[...], wo_ref[W_A + W_B:, :], "x3")
    h2 = _rms(x_ref[...] + upd, gffn_ref[...])
    lg_ref[...] = _mm(h2, wr_ref[...], "x3") + br_ref[...]


def _tail_route(mab_t, mc_t, x_tail, wo, gffn, wr, br):
    return pl.pallas_call(
        _tail_route_kernel,
        out_shape=jax.ShapeDtypeStruct((NB * TAIL, 128), F32),
        compiler_params=pltpu.CompilerParams(vmem_limit_bytes=VMEM_LIMIT),
        name="tail_route",
    )(mab_t, mc_t, x_tail, wo, gffn, wr, br)


def _route(logits):
    lane = lax.broadcasted_iota(I32, logits.shape, 1).astype(F32)
    big = jnp.float32(1e9)
    gl = jnp.where(lane < NG, logits, NEG)
    gmax = jnp.max(gl, axis=-1, keepdims=True)
    p_g = 1.0 / jnp.sum(jnp.exp(gl - gmax), axis=-1, keepdims=True)
    g_idx = jnp.min(jnp.where(gl == gmax, lane, big), axis=-1, keepdims=True)
    lo = NG + EPG * g_idx
    el = jnp.where((lane >= lo) & (lane < lo + EPG), logits, NEG)
    m1 = jnp.max(el, axis=-1, keepdims=True)
    i1 = jnp.min(jnp.where(el == m1, lane, big), axis=-1, keepdims=True)
    zsum = jnp.sum(jnp.exp(el - m1), axis=-1, keepdims=True)
    el2 = jnp.where(lane == i1, NEG, el)
    m2 = jnp.max(el2, axis=-1, keepdims=True)
    i2 = jnp.min(jnp.where(el2 == m2, lane, big), axis=-1, keepdims=True)
    p1 = 1.0 / zsum
    p2 = jnp.exp(m2 - m1) / zsum
    den = p1 + p2
    w1 = p_g * p1 / den
    w2 = p_g * p2 / den
    e1 = i1 - NG
    e2 = i2 - NG
    info = jnp.where(lane == 0, e1,
                     jnp.where(lane == 1, e2,
                               jnp.where(lane == 2, w1, jnp.where(lane == 3, w2, 0.0))))
    return info, e1, e2


def _out_route_kernel(*refs, mode, tm, use_tail):
    (mab_ref, mc_ref, x_ref, wo_ref, gffn_ref, wr_ref, br_ref, cin_ref) = refs[:8]
    if use_tail:
        tail_ref, xmid_ref, h2_ref, info_ref, cnt_ref, carry, lg_s = refs[8:]
    else:
        xmid_ref, h2_ref, info_ref, cnt_ref, carry = refs[8:]

    @pl.when(pl.program_id(0) == 0)
    def _():
        carry[...] = cin_ref[...]

    upd = _mm(mab_ref[...], wo_ref[0:W_A + W_B, :], mode) + _mm(mc_ref[...], wo_ref[W_A + W_B:, :], mode)
    xm = x_ref[...] + upd
    xmid_ref[...] = xm
    h2 = _rms(xm, gffn_ref[...])
    for c in range(NCH):
        h2_ref[pl.ds(c, tm, stride=NCH), :] = h2[:, c * 128:(c + 1) * 128]
    logits = _mm(h2, wr_ref[...], mode) + br_ref[...]
    if use_tail:
        tiles_per_seq = L // tm
        lg_s[...] = logits

        @pl.when(pl.program_id(0) % tiles_per_seq == tiles_per_seq - 1)
        def _():
            lg_s[tm - TAIL:tm, :] = tail_ref[...]
        logits = lg_s[...]
    info, e1, e2 = _route(logits)

    lane = lax.broadcasted_iota(I32, (tm, 128), 1).astype(F32)
    hit1 = lane == e1
    hit2 = lane == e2
    both = jnp.where(hit1, 1.0, 0.0) + jnp.where(hit2, 1.0, 0.0)
    earlier = lax.broadcasted_iota(I32, (tm, tm), 0) > lax.broadcasted_iota(I32, (tm, tm), 1)
    before = _dot(jnp.where(earlier, 1.0, 0.0).astype(BF16), both.astype(BF16)) + carry[...]
    r1 = jnp.sum(jnp.where(hit1, before, 0.0), axis=-1, keepdims=True)
    r2 = jnp.sum(jnp.where(hit2, before, 0.0), axis=-1, keepdims=True)
    info_ref[...] = jnp.where(lane == 4, r1, jnp.where(lane == 5, r2, info))
    total = carry[...] + jnp.sum(both, axis=0, keepdims=True)
    carry[...] = total
    cnt_ref[...] = total


def _out_route(mab, mc, x, wo, gffn, wr, br, cnt_in, tail=None, *, mode, tm):
    n_tok = x.shape[0]
    use_tail = tail is not None
    full = lambda shape: pl.BlockSpec(shape, lambda i: (0,) * len(shape))
    tok = lambda w: pl.BlockSpec((tm, w), lambda i: (i, 0))
    in_specs = [tok(W_A + W_B), tok(W_C), tok(D), full((D, D)), full((1, D)), full((D, 128)),
                full((1, 128)), full((1, 128))]
    scratch = [pltpu.VMEM((1, 128), F32)]
    args = [mab, mc, x, wo, gffn, wr, br, cnt_in]
    if use_tail:
        in_specs.append(pl.BlockSpec((None, TAIL, 128), lambda i: (i // (L // tm), 0, 0)))
        scratch.append(pltpu.VMEM((tm, 128), F32))
        args.append(tail)
    return pl.pallas_call(
        functools.partial(_out_route_kernel, mode=mode, tm=tm, use_tail=use_tail),
        grid=(n_tok // tm,),
        in_specs=in_specs,
        out_specs=[tok(D), pl.BlockSpec((tm * NCH, 128), lambda i: (i, 0)), tok(128), full((1, 128))],
        out_shape=[jax.ShapeDtypeStruct((n_tok, D), F32),
                   jax.ShapeDtypeStruct((n_tok * NCH, 128), F32),
                   jax.ShapeDtypeStruct((n_tok, 128), F32),
                   jax.ShapeDtypeStruct((1, 128), F32)],
        scratch_shapes=scratch,
        compiler_params=_cparams(("arbitrary",)),
        name="out_route_" + mode,
    )(*args)


TM_D = 512
DISPATCH_UNROLL = 4


def _dispatch_kernel(pos_ref, zrow_ref, h2p_ref, h2s_ref, xs_ref, zbuf, sem, zsem):
    i = pl.program_id(0)
    last = i == pl.num_programs(0) - 1

    @pl.when(i == 0)
    def _():
        zbuf[...] = jnp.zeros_like(zbuf)

        def zero_copy(e):
            return pltpu.make_async_copy(zbuf, xs_ref.at[pl.ds(zrow_ref[e], TMX)], zsem)
        for e in range(NE + N_SPARE):
            pl.when(zrow_ref[e] >= 0)(lambda e=e: zero_copy(e).start())
        for e in range(NE + N_SPARE):
            pl.when(zrow_ref[e] >= 0)(lambda e=e: zero_copy(e).wait())

    def copy_token(src_ref, r, slot):
        for k in range(2):
            pltpu.make_async_copy(src_ref.at[r], xs_ref.at[pos_ref[slot + k]], sem).start(priority=k)

    def wait_rows(n):
        pltpu.make_async_copy(xs_ref.at[pl.ds(0, n)], xs_ref.at[pl.ds(0, n)], sem).wait()

    def body(j, c):
        for u in range(DISPATCH_UNROLL):
            r = j * DISPATCH_UNROLL + u
            copy_token(h2p_ref, r, 2 * (i * TM_D + r))
        return c
    lax.fori_loop(0, TM_D // DISPATCH_UNROLL, body, 0)

    @pl.when(last)
    def _():
        def sample_body(j, c):
            copy_token(h2s_ref, j, 2 * (T + j))
            return c
        lax.fori_loop(0, NS, sample_body, 0)
        wait_rows(2 * NS)

    wait_rows(2 * TM_D)


def _dispatch(pos, zrow, h2p, h2s):
    grid_spec = pltpu.PrefetchScalarGridSpec(
        num_scalar_prefetch=2,
        grid=(T // TM_D,),
        in_specs=[pl.BlockSpec((TM_D, NCH, 128), lambda i, pos, zrow: (i, 0, 0)),
                  pl.BlockSpec((NS, NCH, 128), lambda i, pos, zrow: (0, 0, 0))],
        out_specs=pl.BlockSpec(memory_space=pl.ANY),
        scratch_shapes=[pltpu.VMEM((TMX, NCH, 128), F32), pltpu.SemaphoreType.DMA(()),
                        pltpu.SemaphoreType.DMA(())],
    )
    return pl.pallas_call(
        _dispatch_kernel,
        grid_spec=grid_spec,
        out_shape=jax.ShapeDtypeStruct((S_PAD, NCH, 128), F32),
        compiler_params=_cparams(("arbitrary",)),
        name="dispatch",
    )(pos, zrow, h2p.reshape(T, NCH, 128), h2s.reshape(NS, NCH, 128))


def _expert_kernel(te_ref, tc_ref, tn_ref, tp_ref, xs_ref, wg_hbm, wu_hbm, wd_hbm, y_ref,
                   wg_f, wu_f, wd_f, wg_b, wu_b, wd_b, sem, *, layer):
    t = pl.program_id(0)
    slot = tp_ref[t]

    def fetch(e, s):
        return (pltpu.make_async_copy(wg_hbm.at[layer * NE + e], wg_f.at[s], sem.at[s, 0]),
                pltpu.make_async_copy(wu_hbm.at[layer * NE + e], wu_f.at[s], sem.at[s, 1]),
                pltpu.make_async_copy(wd_hbm.at[layer * NE + e], wd_f.at[s], sem.at[s, 2]))

    @pl.when(t == 0)
    def _():
        for cp in fetch(te_ref[0], 0):
            cp.start()

    @pl.when((t == 0) | (te_ref[t] != te_ref[jnp.maximum(t - 1, 0)]))
    def _():
        for cp in fetch(te_ref[t], slot):
            cp.wait()

        @pl.when(tn_ref[t] >= 0)
        def _():
            for cp in fetch(tn_ref[t], 1 - slot):
                cp.start()

        wg_b[...] = wg_f[slot].astype(BF16)
        wu_b[...] = wu_f[slot].astype(BF16)
        wd_b[...] = wd_f[slot].astype(BF16)

    @pl.when(tc_ref[t] > 0)
    def _():
        xb = jnp.concatenate([xs_ref[pl.ds(c, TMX, stride=NCH), :] for c in range(NCH)], axis=1).astype(BF16)
        a = _dot(xb, wg_b[...])
        b = _dot(xb, wu_b[...])
        act = (a * _sigmoid(a) * b).astype(BF16)
        y = _dot(act, wd_b[...])
        for c in range(NCH):
            y_ref[pl.ds(c, TMX, stride=NCH), :] = y[:, c * 128:(c + 1) * 128]

    @pl.when(tc_ref[t] == 0)
    def _():
        y_ref[...] = jnp.zeros_like(y_ref)


def _experts(tile_e, tile_cnt, tile_next, tile_par, xs_sorted, wg, wu, wd, layer):
    grid_spec = pltpu.PrefetchScalarGridSpec(
        num_scalar_prefetch=4,
        grid=(NT,),
        in_specs=[pl.BlockSpec((TMX * NCH, 128), lambda t, *_: (t, 0)),
                  pl.BlockSpec(memory_space=pl.ANY), pl.BlockSpec(memory_space=pl.ANY),
                  pl.BlockSpec(memory_space=pl.ANY)],
        out_specs=pl.BlockSpec((TMX * NCH, 128), lambda t, *_: (t, 0)),
        scratch_shapes=[pltpu.VMEM((2, D, DE), F32), pltpu.VMEM((2, D, DE), F32), pltpu.VMEM((2, DE, D), F32),
                        pltpu.VMEM((D, DE), BF16), pltpu.VMEM((D, DE), BF16), pltpu.VMEM((DE, D), BF16),
                        pltpu.SemaphoreType.DMA((2, 3))],
    )
    y = pl.pallas_call(
        functools.partial(_expert_kernel, layer=layer),
        grid_spec=grid_spec,
        out_shape=jax.ShapeDtypeStruct((S_PAD * NCH, 128), F32),
        compiler_params=_cparams(("arbitrary",)),
        name="experts",
    )(tile_e, tile_cnt, tile_next, tile_par, xs_sorted.reshape(S_PAD * NCH, 128), wg, wu, wd)
    return y.reshape(S_PAD, NCH, 128)


def _combine_ple_kernel(pos_ref, xmid_ref, info_ref, p_ref, wple_ref, wpg_ref, gple_ref, gfin_ref, ys_ref,
                        x_ref, y_ref, gbuf, x2s, sem, *, mode, tm, slot0):
    i = pl.program_id(0)
    n = pl.num_programs(0)

    def start_row(tile, buf, j):
        s = slot0 + 2 * (tile * tm + j)
        for k in range(2):
            dst = gbuf.at[buf, pl.ds(pl.multiple_of((k * tm + j) * NCH, NCH), NCH), :]
            pltpu.make_async_copy(ys_ref.at[pos_ref[s + k]], dst, sem.at[buf]).start(priority=k)

    @pl.when(i == 0)
    def _():
        def body(j, c):
            start_row(0, 0, j)
            return c
        lax.fori_loop(0, tm, body, 0)

    buf = i % 2

    def compute(prefetch):
        nxt = 0
        def issue_some(count):
            nonlocal nxt
            if prefetch:
                for j in range(nxt, min(nxt + count, tm)):
                    start_row(i + 1, 1 - buf, j)
                nxt = min(nxt + count, tm)

        pltpu.make_async_copy(gbuf.at[buf], gbuf.at[buf], sem.at[buf]).wait()
        info = info_ref[...]
        w1 = info[:, 2:3]
        w2 = info[:, 3:4]
        for c in range(8):
            cs = slice(c * 128, (c + 1) * 128)
            y1 = gbuf[buf, pl.ds(c, tm, stride=NCH), :]
            y2 = gbuf[buf, pl.ds(tm * NCH + c, tm, stride=NCH), :]
            x2s[:, cs] = xmid_ref[:, cs] + w1 * y1 + w2 * y2
            issue_some(tm // 16)
        x2 = x2s[...]
        gate = _sigmoid(_mm(_rms(x2, gple_ref[...]), wpg_ref[...], mode))
        issue_some(tm // 4)
        x3 = x2 + _mm(p_ref[...], wple_ref[...], mode) * gate
        issue_some(tm)
        x_ref[...] = x3
        y_ref[...] = _rms(x3, gfin_ref[...])

    @pl.when(i + 1 < n)
    def _():
        compute(True)

    @pl.when(i + 1 == n)
    def _():
        compute(False)


def _combine_ple(pos, xmid, info, p, wple, wpg, gple, gfin, y_sorted, *, mode, tm, slot0):
    n_tok = xmid.shape[0]
    full = lambda shape: pl.BlockSpec(shape, lambda i, pos: (0,) * len(shape))
    tok = lambda w: pl.BlockSpec((tm, w), lambda i, pos: (i, 0))
    grid_spec = pltpu.PrefetchScalarGridSpec(
        num_scalar_prefetch=1,
        grid=(n_tok // tm,),
        in_specs=[tok(D), tok(128), tok(D_PLE), full((D_PLE, D)), full((D, D)), full((1, D)), full((1, D)),
                  pl.BlockSpec(memory_space=pl.ANY)],
        out_specs=[tok(D), tok(D)],
        scratch_shapes=[pltpu.VMEM((2, 2 * tm * NCH, 128), F32), pltpu.VMEM((tm, D), F32),
                        pltpu.SemaphoreType.DMA((2,))],
    )
    return pl.pallas_call(
        functools.partial(_combine_ple_kernel, mode=mode, tm=tm, slot0=slot0),
        grid_spec=grid_spec,
        out_shape=[jax.ShapeDtypeStruct((n_tok, D), F32), jax.ShapeDtypeStruct((n_tok, D), F32)],
        compiler_params=_cparams(("arbitrary",)),
        name="combine_ple_" + mode,
    )(pos, xmid, info, p, wple, wpg, gple, gfin, y_sorted)


def _sample_in_kernel(x_ref, gmix_ref, win_ref, convw_ref, s0_ref, s1_ref, lng_ref, lnb_ref,
                      ws0_ref, bs0_ref, bf_ref, gout_ref,
                      mab_ref, q_ref, k_ref, v_ref, logf_ref, z_ref, vn_ref):
    h = _rms(x_ref[...], gmix_ref[...])
    h_hi, h_lo = _split(h)

    def proj(lo, hi):
        w_hi, w_lo = _split(win_ref[:, lo:hi])
        return _dot(h_hi, w_hi) + _dot(h_lo, w_hi) + _dot(h_hi, w_lo)

    pa = proj(0, 3 * W_A)
    z = pa[:, W_A:2 * W_A] * pa[:, 2 * W_A:3 * W_A]
    cw = convw_ref[...]
    y_a = pa[:, 0:W_A] * (cw[0:1, :] * s0_ref[...] + cw[1:2, :] * s1_ref[...] + cw[2:3, :] * z)
    z_ref[...] = z
    gout = gout_ref[...]
    mab_ref[:, 0:W_A] = _rms(y_a, gout[:, 0:W_A])

    o = 3 * W_A
    pb = proj(o, o + 2 * W_B)
    v_b = pb[:, W_B:2 * W_B]
    mu = jnp.mean(v_b, axis=-1, keepdims=True)
    vc = v_b - mu
    var = jnp.mean(vc * vc, axis=-1, keepdims=True)
    vn = vc * lax.rsqrt(var + EPS) * lng_ref[...] + lnb_ref[...]
    vn_ref[...] = vn
    y_b = pb[:, 0:W_B] * (ws0_ref[...] * vn + bs0_ref[...])
    mab_ref[:, W_A:W_A + W_B] = _rms(y_b, gout[:, W_A:W_A + W_B])

    o = 3 * W_A + 2 * W_B
    q_ref[...] = proj(o, o + W_C) * (HD ** -0.5)
    k_ref[...] = proj(o + W_C, o + 2 * W_C)
    v_ref[...] = proj(o + 2 * W_C, o + 3 * W_C)
    f = proj(o + 3 * W_C, D_IN_PAD)
    logf_ref[...] = _log_sigmoid(f + bf_ref[...])


def _sample_in(x, gmix, win_pad, convw, s0, s1, lng, lnb, ws0, bs0, bf_pad, gout):
    shapes = [(NS, W_A + W_B), (NS, W_C), (NS, W_C), (NS, W_C), (NS, 128), (NS, W_A), (NS, W_B)]
    return pl.pallas_call(
        _sample_in_kernel,
        out_shape=[jax.ShapeDtypeStruct(s, F32) for s in shapes],
        compiler_params=pltpu.CompilerParams(vmem_limit_bytes=VMEM_LIMIT),
        name="sample_in",
    )(x, gmix, win_pad, convw, s0, s1, lng, lnb, ws0, bs0, bf_pad, gout)


def _sample_attn_kernel(pt_ref, qb_ref, q8_ref, kn_ref, vnt_ref, lfn_ref, gct_ref, *rest):
    k_refs = rest[0:PP]
    v_refs = rest[PP:2 * PP]
    lf_refs = rest[2 * PP:3 * PP]
    o_ref = rest[3 * PP]
    m_s, l_s, acc_s, tot_s = rest[3 * PP + 1:]
    j = pl.program_id(1)

    @pl.when(j == 0)
    def _():
        m_s[...] = jnp.full_like(m_s, NEG)
        l_s[...] = jnp.zeros_like(l_s)
        acc_s[...] = jnp.zeros_like(acc_s)
        tot_s[...] = jnp.zeros_like(tot_s)

    cn = lfn_ref[...]
    lane = lax.broadcasted_iota(I32, (H_C, PAGE), 1)

    tot = tot_s[...]
    scores = []
    for pp in reversed(range(PP)):
        lf = lf_refs[pp][...]
        inc = lf
        sh = 1
        while sh < PAGE:
            inc = inc + jnp.where(lane >= sh, pltpu.roll(inc, sh, axis=1), 0.0)
            sh *= 2
        page_tot = inc[:, PAGE - 1:PAGE]
        bias = (page_tot - inc) + tot + cn
        tot = tot + page_tot
        rows = [jnp.sum(k_refs[pp][h] * qb_ref[h], axis=0, keepdims=True) for h in range(H_C)]
        scores.append((pp, jnp.concatenate(rows, axis=0) + bias))
    tot_s[...] = tot

    m_old = m_s[...]
    m = m_old
    for _, s in scores:
        m = jnp.maximum(m, jnp.max(s, axis=-1, keepdims=True))
    a = jnp.exp(m_old - m)
    l = a * l_s[...]
    probs = []
    for pp, s in scores:
        p = jnp.exp(s - m)
        l = l + jnp.sum(p, axis=-1, keepdims=True)
        probs.append((pp, p))
    m_s[...] = m
    l_s[...] = l
    for h in range(H_C):
        acc_h = acc_s[h] * a[h:h + 1, :]
        for pp, p in probs:
            acc_h = acc_h + v_refs[pp][h] * p[h:h + 1, :]
        acc_s[h] = acc_h

    @pl.when(j == NPG - 1)
    def _():
        s_new = jnp.sum(q8_ref[...] * kn_ref[...], axis=-1, keepdims=True)
        mn = jnp.maximum(m, s_new)
        a2 = jnp.exp(m - mn)
        pn = jnp.exp(s_new - mn)
        lt = a2 * l + pn
        outs = []
        ms = jnp.zeros((1, 1), F32)
        for h in range(H_C):
            hs = slice(h, h + 1)
            o_h = (a2[hs, :] * jnp.sum(acc_s[h], axis=1, keepdims=True) + pn[hs, :] * vnt_ref[h]) / lt[hs, :]
            ms = ms + jnp.sum(o_h * o_h, axis=0, keepdims=True)
            outs.append(o_h)
        scale = lax.rsqrt(ms / W_C + EPS)
        for h in range(H_C):
            o_ref[h] = outs[h] * scale * gct_ref[h]


def _sample_attn(pt_flat, qb, q8, kn8, vnt, lfn3, gct, ckt, cvt, clft, layer):
    def page_map(nd):
        return lambda pp: (lambda n, j, pt: (layer, pt[n * N_PAGES + (NPG - 1 - j) * PP + pp]) + (0,) * nd)

    head = pl.BlockSpec((None, H_C, HD), lambda n, j, pt: (n, 0, 0))
    col = pl.BlockSpec((None, H_C, HD, 1), lambda n, j, pt: (n, 0, 0, 0))
    in_specs = [pl.BlockSpec((None, H_C, HD, PAGE), lambda n, j, pt: (n, 0, 0, 0)),
                head, head, col,
                pl.BlockSpec((None, H_C, 1), lambda n, j, pt: (n, 0, 0)),
                pl.BlockSpec((H_C, HD, 1), lambda n, j, pt: (0, 0, 0))]
    in_specs += [pl.BlockSpec((None, None, H_C, HD, PAGE), page_map(3)(pp)) for pp in range(PP)]
    in_specs += [pl.BlockSpec((None, None, H_C, HD, PAGE), page_map(3)(pp)) for pp in range(PP)]
    in_specs += [pl.BlockSpec((None, None, H_C, PAGE), page_map(2)(pp)) for pp in range(PP)]
    grid_spec = pltpu.PrefetchScalarGridSpec(
        num_scalar_prefetch=1,
        grid=(NS, NPG),
        in_specs=in_specs,
        out_specs=col,
        scratch_shapes=[pltpu.VMEM((H_C, 1), F32), pltpu.VMEM((H_C, 1), F32),
                        pltpu.VMEM((H_C, HD, PAGE), F32), pltpu.VMEM((H_C, 1), F32)],
    )
    return pl.pallas_call(
        _sample_attn_kernel,
        grid_spec=grid_spec,
        out_shape=jax.ShapeDtypeStruct((NS, H_C, HD, 1), F32),
        compiler_params=_cparams(("arbitrary", "arbitrary")),
        name="sample_attn",
    )(pt_flat, qb, q8, kn8, vnt, lfn3, gct, *([ckt] * PP), *([cvt] * PP), *([clft] * PP))


def _schedule(cnt_f, info_all):
    cnt = cnt_f[0, :NE].astype(I32)
    ntile = (cnt + TMX - 1) // TMX
    tcum = jnp.cumsum(ntile).astype(I32)
    tbase = tcum - ntile
    t = jnp.arange(NT, dtype=I32)
    ids = jnp.arange(NE, dtype=I32)
    last_e = jnp.max(jnp.where(ntile > 0, ids, 0))
    tile_e = jnp.minimum(jnp.sum((tcum[None, :] <= t[:, None]).astype(I32), axis=1), last_e)
    later = (ids[None, :] > ids[:, None]) & (ntile[None, :] > 0)
    next_e = jnp.min(jnp.where(later, ids[None, :], NE), axis=1)
    next_e = jnp.where(next_e < NE, next_e, -1)
    run = jnp.cumsum((ntile > 0).astype(I32)) - 1
    sel = tile_e[:, None] == ids[None, :]
    tile_next = jnp.sum(jnp.where(sel, next_e[None, :], 0), axis=1).astype(I32)
    tile_par = (jnp.sum(jnp.where(sel, run[None, :], 0), axis=1) % 2).astype(I32)
    within = t - jnp.sum(jnp.where(sel, tbase[None, :], 0), axis=1)
    tile_cnt = jnp.clip(jnp.sum(jnp.where(sel, cnt[None, :], 0), axis=1) - within * TMX, 0, TMX).astype(I32)
    eid = info_all[:, 0:2].astype(I32)
    rank = info_all[:, 4:6].astype(I32)
    base = jnp.sum(jnp.where(eid[:, :, None] == jnp.arange(NE, dtype=I32)[None, None, :],
                             (tbase * TMX)[None, None, :], 0), axis=-1)
    pos = (base + rank).reshape(-1).astype(I32)
    last_tile = jnp.where(ntile > 0, tcum - 1, -1)
    spare = tcum[NE - 1] + jnp.arange(N_SPARE, dtype=I32)
    ztile = jnp.concatenate([last_tile, jnp.where(spare < NT, spare, -1)])
    zrow = jnp.where(ztile >= 0, ztile * TMX, -1).astype(I32)
    return (tile_e, tile_cnt, tile_next, tile_par), pos, zrow


def kernel(x_prompt, x_sample, p_prompt, p_sample, cache_k, cache_v, cache_logf, state_conv, page_table, g_mix, w_in, conv_w, ln_g, ln_b, w_s, b_s, b_f, g_out, w_o, g_ffn, w_grp, b_grp, w_rt, b_rt, w_gate, w_up, w_down, g_ple, w_ple, w_ple_gate, g_final):
    n_pool = cache_k.shape[1]
    xp = x_prompt.reshape(T, D)
    xs = x_sample.reshape(NS, D)
    pt_flat = page_table.reshape(-1).astype(I32)
    del n_pool
    ckt = jnp.transpose(cache_k, (0, 1, 3, 4, 2))
    cvt = jnp.transpose(cache_v, (0, 1, 3, 4, 2))
    clft = jnp.swapaxes(cache_logf, 2, 3)
    zero_cnt = jnp.zeros((1, 128), F32)
    wg_all = w_gate.reshape(DEPTH * NE, D, DE)
    wu_all = w_up.reshape(DEPTH * NE, D, DE)
    wd_all = w_down.reshape(DEPTH * NE, DE, D)
    gfin = g_final.reshape(1, D)

    outs = {k: [] for k in ("kp", "vp", "lfp", "cvp", "ks", "vs", "lfs", "cvs", "chv")}
    yp = ys = None
    for i in range(DEPTH):
        gmix = g_mix[i].reshape(1, D)
        gout = g_out[i].reshape(1, D)
        gffn = g_ffn[i].reshape(1, D)
        gple = g_ple[i].reshape(1, D)
        win_pad = jnp.pad(w_in[i], ((0, 0), (0, D_IN_PAD - D_IN)))
        win_bf = win_pad.astype(BF16)
        bf_pad = jnp.pad(b_f[i], (0, 128 - H_C)).reshape(1, 128)
        lng = ln_g[i].reshape(1, W_B)
        lnb = ln_b[i].reshape(1, W_B)
        bse = jnp.repeat(b_s[i].T, HD, axis=1)
        ws0 = jnp.repeat(w_s[i][:, 0, 0], HD).reshape(1, W_B)
        bs0 = jnp.repeat(b_s[i][:, 0], HD).reshape(1, W_B)
        wr = jnp.pad(jnp.concatenate([w_grp[i], w_rt[i].reshape(D, NE)], axis=1), ((0, 0), (0, 128 - NG - NE)))
        br = jnp.pad(jnp.concatenate([b_grp[i], b_rt[i].reshape(NE)]), (0, 128 - NG - NE)).reshape(1, 128)

        first = i == 0
        mab, qbf, kbf, vbf, k_p, v_p, lf_p, c_p, ctail = _prompt_in(
            xp, gmix, win_bf, conv_w[i], lng, lnb, w_s[i], bse, bf_pad, gout,
            win_kv=win_pad[:, KV_COL0:] if first else None)
        ct = jnp.swapaxes(c_p[:, :H_C].reshape(NB, L, H_C), 1, 2)
        mc = _prompt_attn(qbf, kbf, vbf, ct, gout)
        tail = None
        if first:
            x_seq = xp.reshape(NB, L, D)
            mab_t, q_t = _tail_proj(x_seq[:, L - CHUNK:].reshape(NB * CHUNK, D), gmix,
                                    win_pad[:, :3 * W_A + 2 * W_B + W_C], conv_w[i], lng, lnb, w_s[i], bse, gout)
            mc_t = _tail_attn(q_t, k_p, v_p, c_p, gout)
            tail = _tail_route(mab_t, mc_t, x_seq[:, L - TAIL:].reshape(NB * TAIL, D), w_o[i], gffn, wr,
                               br).reshape(NB, TAIL, 128)
        xmid_p, h2_p, info_p, cnt_p = _out_route(mab, mc, xp, w_o[i].astype(BF16), gffn, wr.astype(BF16), br,
                                                 zero_cnt, tail, mode="bf16", tm=TM_C)

        mab_s, q_s, k_s, v_s, lf_s128, z_s, vn_s = _sample_in(
            xs, gmix, win_pad, conv_w[i], state_conv[i, :, 0], state_conv[i, :, 1], lng, lnb, ws0, bs0,
            bf_pad, gout)
        lf_s = lf_s128[:, :H_C]
        q8 = q_s.reshape(NS, H_C, HD)
        mc_s = _sample_attn(pt_flat, jnp.broadcast_to(q8[..., None], (NS, H_C, HD, PAGE)), q8,
                            k_s.reshape(NS, H_C, HD), v_s.reshape(NS, H_C, HD, 1), lf_s.reshape(NS, H_C, 1),
                            gout[:, W_A + W_B:].reshape(H_C, HD, 1), ckt, cvt, clft, i)
        xmid_s, h2_s, info_s, cnt_all = _out_route(mab_s, mc_s.reshape(NS, W_C), xs, w_o[i], gffn, wr, br,
                                                   cnt_p, mode="x3", tm=NS)

        tiles, pos, zrow = _schedule(cnt_all, jnp.concatenate([info_p[:, :8], info_s[:, :8]], axis=0))
        xs_sorted = _dispatch(pos, zrow, h2_p, h2_s)
        y_sorted = _experts(*tiles, xs_sorted, wg_all, wu_all, wd_all, i)

        xp, yp = _combine_ple(pos, xmid_p, info_p, p_prompt[i].reshape(T, D_PLE), w_ple[i].astype(BF16),
                              w_ple_gate[i].astype(BF16), gple, gfin, y_sorted, mode="bf16", tm=TM_F, slot0=0)
        xs, ys = _combine_ple(pos, xmid_s, info_s, p_sample[i].reshape(NS, D_PLE), w_ple[i],
                              w_ple_gate[i], gple, gfin, y_sorted, mode="x3", tm=NS, slot0=2 * T)

        outs["kp"].append(k_p.reshape(NB, L, H_C, HD))
        outs["vp"].append(v_p.reshape(NB, L, H_C, HD))
        outs["lfp"].append(lf_p.reshape(NB, L, H_C))
        outs["cvp"].append(ctail)
        outs["ks"].append(k_s.reshape(NS, 1, H_C, HD))
        outs["vs"].append(v_s.reshape(NS, 1, H_C, HD))
        outs["lfs"].append(lf_s.reshape(NS, 1, H_C))
        outs["cvs"].append(jnp.stack([state_conv[i, :, 1], z_s], axis=1))
        outs["chv"].append(vn_s.reshape(NS, 1, W_B))

    st = lambda k: jnp.stack(outs[k])
    return (yp.reshape(NB, L, D), ys.reshape(NS, 1, D), st("kp"), st("vp"), st("lfp"), st("cvp"),
            st("ks"), st("vs"), st("lfs"), st("cvs"), st("chv"))
```
